```python
import math
import jax, jax.numpy as jnp
from jax import lax
import numpy as np

D_MODEL = 1024
BATCH = 8
SEQ = 4096
DEPTH = 1

GRID_W = 64
CTX_LEN = 256
N_FOURIER_GROUPS = 4
FOURIER_GROUP_DIM = 128
FOURIER_WIDTH = N_FOURIER_GROUPS * FOURIER_GROUP_DIM
DN_HEADS = 8
DN_HEAD_DIM = 128
DN_WIDTH = DN_HEADS * DN_HEAD_DIM
DN_CONV = 3
DN_CHUNK = 64
N_DIR = 2
N_BRANCH = 2
D_FF = 2816
FFN_CONV = 3
NORM_EPS = 1e-6
L2_EPS = 1e-6

OFF_F = 0
OFF_Q = OFF_F + FOURIER_WIDTH
OFF_K = OFF_Q + DN_WIDTH
OFF_V = OFF_K + DN_WIDTH
OFF_Z = OFF_V + DN_WIDTH
OFF_B = OFF_Z + DN_WIDTH
OFF_A = OFF_B + N_DIR * DN_HEADS
OFF_G = OFF_A + N_DIR * DN_HEADS
IN_WIDTH = OFF_G + N_BRANCH * D_MODEL

kernel_name = "hybrid_fourier_deltanet_convffn_dit"


def rms_norm(x, w):
    xf = x.astype(jnp.float32)
    y = xf * lax.rsqrt(jnp.mean(xf * xf, axis=-1, keepdims=True) + NORM_EPS)
    return (y * w.astype(jnp.float32)).astype(x.dtype)


def l2norm(x):
    return x * lax.rsqrt(jnp.sum(x * x, axis=-1, keepdims=True) + L2_EPS)


def modulate(h, shift, scale):
    return h * (1 + scale) + shift


def dwconv1d(x, w):
    k = w.shape[0]
    pad = (k - 1) // 2
    return lax.conv_general_dilated(
        x, w[:, None, :].astype(x.dtype), window_strides=(1,), padding=[(pad, pad)],
        dimension_numbers=('NWC', 'WIO', 'NWC'), feature_group_count=x.shape[-1])


def dwconv2d(x, w, rows, cols):
    b, t, ch = x.shape
    kh, kw = w.shape[0], w.shape[1]
    y = lax.conv_general_dilated(
        x.reshape(b, rows, cols, ch), w[:, :, None, :].astype(x.dtype), window_strides=(1, 1),
        padding=[((kh - 1) // 2, (kh - 1) // 2), ((kw - 1) // 2, (kw - 1) // 2)],
        dimension_numbers=('NHWC', 'HWIO', 'NHWC'), feature_group_count=ch)
    return y.reshape(b, t, ch)


def fourier_mix(f):
    b, t, _ = f.shape
    fg = f.astype(jnp.float32).reshape(b, t, N_FOURIER_GROUPS, FOURIER_GROUP_DIM)
    out = jnp.fft.fft2(fg, axes=(1, 3), norm="ortho").real
    return out.reshape(b, t, FOURIER_WIDTH).astype(f.dtype)


def chunk_gated_delta(q, k, v, g, beta, state):
    b, t, h, dk = q.shape
    dv = v.shape[-1]
    n = t // DN_CHUNK

    def chunks(a):
        a = a.reshape((b, n, DN_CHUNK, h) + a.shape[3:])
        return jnp.moveaxis(a, (1, 3), (0, 2))

    qc, kc, vc = chunks(q), chunks(k), chunks(v)
    gc = jnp.cumsum(chunks(g), axis=-1)
    bc = chunks(beta)
    idx = jnp.arange(DN_CHUNK)
    lower_strict = idx[:, None] > idx[None, :]
    lower_incl = idx[:, None] >= idx[None, :]
    decay = jnp.exp(jnp.where(lower_incl, gc[..., :, None] - gc[..., None, :], -jnp.inf))
    kk = jnp.einsum('nbhcd,nbhsd->nbhcs', kc, kc)
    l_mat = jnp.where(lower_strict, bc[..., :, None] * kk * decay, 0.0)
    rhs = jnp.concatenate([vc * bc[..., None], kc * (bc * jnp.exp(gc))[..., None]], axis=-1)
    sol = lax.linalg.triangular_solve(l_mat, rhs, left_side=True, lower=True, unit_diagonal=True)
    u, w = sol[..., :dv], sol[..., dv:]
    qk = jnp.einsum('nbhcd,nbhsd->nbhcs', qc, kc) * decay
    g_last = gc[..., -1]
    k_tail = kc * jnp.exp(g_last[..., None] - gc)[..., None]
    q_head = qc * jnp.exp(gc)[..., None]

    def step(s, xs):
        q_h, qk_i, u_i, w_i, k_t, gl = xs
        v_new = u_i - jnp.einsum('bhcd,bhde->bhce', w_i, s)
        o = jnp.einsum('bhcd,bhde->bhce', q_h, s) + jnp.einsum('bhcs,bhse->bhce', qk_i, v_new)
        s = s * jnp.exp(gl)[..., None, None] + jnp.einsum('bhcd,bhce->bhde', k_t, v_new)
        return s, o

    state, o = lax.scan(step, state, (q_head, qk, u, w, k_tail, g_last))
    o = jnp.moveaxis(o, (0, 2), (1, 3)).reshape(b, t, h, dv)
    return o, state


def mixer_core(h, init_states, w_in, conv_qkv, a_log, dt_bias):
    b, t, _ = h.shape
    p = h @ w_in
    f_in = p[..., OFF_F:OFF_Q]
    qkv = jax.nn.silu(dwconv1d(p[..., OFF_Q:OFF_Z], conv_qkv)).astype(jnp.float32)
    q, k, v = [a.reshape(b, t, DN_HEADS, DN_HEAD_DIM) for a in jnp.split(qkv, 3, axis=-1)]
    q = l2norm(q) * (DN_HEAD_DIM ** -0.5)
    k = l2norm(k)
    z = p[..., OFF_Z:OFF_B]
    beta = jax.nn.sigmoid(p[..., OFF_B:OFF_A].astype(jnp.float32)).reshape(b, t, N_DIR, DN_HEADS)
    a = p[..., OFF_A:OFF_G].astype(jnp.float32).reshape(b, t, N_DIR, DN_HEADS)
    g = -jnp.exp(a_log.astype(jnp.float32)) * jax.nn.softplus(a + dt_bias.astype(jnp.float32))
    gates = jax.nn.sigmoid(p[..., OFF_G:]).reshape(b, t, N_BRANCH, D_MODEL)
    flip = lambda arr: jnp.flip(arr, axis=1)
    o_f, s_f = chunk_gated_delta(q, k, v, g[:, :, 0], beta[:, :, 0], init_states[0])
    o_b, s_b = chunk_gated_delta(flip(q), flip(k), flip(v), flip(g[:, :, 1]), flip(beta[:, :, 1]),
                                 init_states[1])
    return f_in, o_f + flip(o_b), z, gates, (s_f, s_b)


def merge_branches(f_in, o_dn, z, gates, dn_norm, w_fourier, w_dn, w_out):
    b, t, _ = z.shape
    zh = z.reshape(b, t, DN_HEADS, DN_HEAD_DIM).astype(jnp.float32)
    o = (rms_norm(o_dn, dn_norm) * jax.nn.silu(zh)).astype(z.dtype).reshape(b, t, DN_WIDTH)
    y_f = fourier_mix(f_in) @ w_fourier
    y_d = o @ w_dn
    return (gates[:, :, 0] * y_f + gates[:, :, 1] * y_d) @ w_out


def ffn_sublayer(stream, shift, scale, gate, g_pre, g_post, w_up, conv_w, w_down, rows, cols):
    h = modulate(rms_norm(stream, g_pre), shift, scale)
    a, u = jnp.split(h @ w_up, 2, axis=-1)
    a = dwconv2d(a, conv_w, rows, cols)
    y = (jax.nn.silu(a) * u) @ w_down
    return stream + gate * rms_norm(y, g_post)


def setup_inputs(seed: int = 0) -> dict:
    key = jax.random.key(seed)
    ks = jax.random.split(key, 24)
    nrm = lambda k, shape, scale: jax.random.normal(k, shape, jnp.float32) * scale
    gain = lambda k, n: 1.0 + nrm(k, (DEPTH, n), 0.02)
    x = nrm(ks[0], (BATCH, SEQ, D_MODEL), 1.0)
    c = nrm(ks[1], (BATCH, D_MODEL), 1.0)
    ctx = nrm(ks[2], (BATCH, CTX_LEN, D_MODEL), 1.0)
    c_ctx = nrm(ks[3], (D_MODEL,), 1.0)
    w_ada = nrm(ks[4], (DEPTH, D_MODEL, 6 * D_MODEL), 0.5 * D_MODEL ** -0.5)
    b_ada = nrm(ks[5], (DEPTH, 6 * D_MODEL), 0.01)
    norm_pre_mix = gain(ks[6], D_MODEL)
    norm_post_mix = gain(ks[7], D_MODEL)
    norm_pre_ffn = gain(ks[8], D_MODEL)
    norm_post_ffn = gain(ks[9], D_MODEL)
    w_in = nrm(ks[10], (DEPTH, D_MODEL, IN_WIDTH), D_MODEL ** -0.5)
    conv_qkv = nrm(ks[11], (DEPTH, DN_CONV, 3 * DN_WIDTH), DN_CONV ** -0.5)
    a_log = jnp.log(jax.random.uniform(ks[12], (DEPTH, N_DIR, DN_HEADS), jnp.float32, 1.0, 16.0))
    dt = jnp.exp(jax.random.uniform(ks[13], (DEPTH, N_DIR, DN_HEADS), jnp.float32,
                                    math.log(1e-3), math.log(1e-1)))
    dt_bias = dt + jnp.log(-jnp.expm1(-dt))
    dn_norm = gain(ks[14], DN_HEAD_DIM)
    w_fourier = nrm(ks[15], (DEPTH, FOURIER_WIDTH, D_MODEL), FOURIER_WIDTH ** -0.5)
    w_dn = nrm(ks[16], (DEPTH, DN_WIDTH, D_MODEL), DN_WIDTH ** -0.5)
    w_out = nrm(ks[17], (DEPTH, D_MODEL, D_MODEL), D_MODEL ** -0.5)
    w_up = nrm(ks[18], (DEPTH, D_MODEL, 2 * D_FF), D_MODEL ** -0.5)
    conv_ffn = nrm(ks[19], (DEPTH, FFN_CONV, FFN_CONV, D_FF), 1.0 / FFN_CONV)
    w_down = nrm(ks[20], (DEPTH, D_FF, D_MODEL), D_FF ** -0.5)
    return {"x": x, "c": c, "ctx": ctx, "c_ctx": c_ctx, "w_ada": w_ada, "b_ada": b_ada,
            "norm_pre_mix": norm_pre_mix, "norm_post_mix": norm_post_mix,
            "norm_pre_ffn": norm_pre_ffn, "norm_post_ffn": norm_post_ffn,
            "w_in": w_in, "conv_qkv": conv_qkv, "a_log": a_log, "dt_bias": dt_bias,
            "dn_norm": dn_norm, "w_fourier": w_fourier, "w_dn": w_dn, "w_out": w_out,
            "w_up": w_up, "conv_ffn": conv_ffn, "w_down": w_down}


def reference(x, c, ctx, c_ctx, w_ada, b_ada, norm_pre_mix, norm_post_mix, norm_pre_ffn,
              norm_post_ffn, w_in, conv_qkv, a_log, dt_bias, dn_norm, w_fourier, w_dn, w_out,
              w_up, conv_ffn, w_down):
    bsz, seq, _ = x.shape
    rows = seq // GRID_W
    ctx_len = ctx.shape[1]
    for l in range(DEPTH):
        mod_x = (jax.nn.silu(c) @ w_ada[l] + b_ada[l])[:, None, :]
        mod_c = (jax.nn.silu(c_ctx) @ w_ada[l] + b_ada[l])[None, None, :]
        sh1, sc1, gt1, sh2, sc2, gt2 = jnp.split(mod_x, 6, axis=-1)
        csh1, csc1, cgt1, csh2, csc2, cgt2 = jnp.split(mod_c, 6, axis=-1)
        zero = jnp.zeros((bsz, DN_HEADS, DN_HEAD_DIM, DN_HEAD_DIM), jnp.float32)

        hc = modulate(rms_norm(ctx, norm_pre_mix[l]), csh1, csc1)
        fc, oc, zc, gtc, ctx_states = mixer_core(hc, (zero, zero), w_in[l], conv_qkv[l],
                                                 a_log[l], dt_bias[l])

        hx = modulate(rms_norm(x, norm_pre_mix[l]), sh1, sc1)
        fx, ox, zx, gtx, _ = mixer_core(hx, ctx_states, w_in[l], conv_qkv[l], a_log[l], dt_bias[l])
        yx = merge_branches(fx, ox, zx, gtx, dn_norm[l], w_fourier[l], w_dn[l], w_out[l])
        x = x + gt1 * rms_norm(yx, norm_post_mix[l])
        x = ffn_sublayer(x, sh2, sc2, gt2, norm_pre_ffn[l], norm_post_ffn[l], w_up[l],
                         conv_ffn[l], w_down[l], rows, GRID_W)

        if l < DEPTH - 1:
            yc = merge_branches(fc, oc, zc, gtc, dn_norm[l], w_fourier[l], w_dn[l], w_out[l])
            ctx = ctx + cgt1 * rms_norm(yc, norm_post_mix[l])
            ctx = ffn_sublayer(ctx, csh2, csc2, cgt2, norm_pre_ffn[l], norm_post_ffn[l], w_up[l],
                               conv_ffn[l], w_down[l], 1, ctx_len)
    return x
```

```python
import functools
import math

import numpy as np
import jax
import jax.numpy as jnp
from jax import lax
from jax.experimental import pallas as pl
from jax.experimental.pallas import tpu as pltpu

F32 = jnp.float32
BF16 = jnp.bfloat16

GRID_W = 64
N_GROUPS = 4
GROUP_DIM = 128
F_WIDTH = N_GROUPS * GROUP_DIM
HEADS = 8
HEAD_DIM = 128
DN_WIDTH = HEADS * HEAD_DIM
N_DIR = 2
NORM_EPS = 1e-6
L2_EPS = 1e-6
CHUNK = 128
LOG2_CHUNK = 7
FFT_N1 = 8
VMEM_LIMIT = 56 * 1024 * 1024


def _cparams(*sem):
    return pltpu.CompilerParams(dimension_semantics=sem, vmem_limit_bytes=VMEM_LIMIT)


def _sigmoid(x):
    return 1.0 / (1.0 + jnp.exp(-x))


def _silu(x):
    return x * _sigmoid(x)


def _softplus(x):
    return jnp.maximum(x, 0.0) + jnp.log1p(jnp.exp(-jnp.abs(x)))


def _dot(a, b):
    return jnp.dot(a, b, preferred_element_type=F32)


def _dot_nt(a, b):
    return lax.dot_general(a, b, (((1,), (1,)), ((), ())), preferred_element_type=F32)


def _split3(x):
    x1 = x.astype(BF16)
    r1 = x - x1.astype(F32)
    x2 = r1.astype(BF16)
    x3 = (r1 - x2.astype(F32)).astype(BF16)
    return x1, x2, x3


def _resident(shape):
    nd = len(shape)
    return pl.BlockSpec(shape, lambda *_: (0,) * nd, pipeline_mode=pl.Buffered(1))


def _ada_kernel(c_ref, w_ref, b_ref, o_ref):
    s = _silu(c_ref[...])
    o_ref[...] = jnp.dot(s, w_ref[...], preferred_element_type=F32,
                         precision=lax.Precision.HIGHEST) + b_ref[...]


def _ada(c_rows, w_ada, b_ada):
    rows, d = c_rows.shape
    n = w_ada.shape[1]
    tn = n // 4
    return pl.pallas_call(
        _ada_kernel,
        grid=(n // tn,),
        in_specs=[pl.BlockSpec((rows, d), lambda j: (0, 0)),
                  pl.BlockSpec((d, tn), lambda j: (0, j)),
                  pl.BlockSpec((1, tn), lambda j: (0, j))],
        out_specs=pl.BlockSpec((rows, tn), lambda j: (0, j)),
        out_shape=jax.ShapeDtypeStruct((rows, n), F32),
        compiler_params=_cparams("arbitrary"),
        name="ada",
    )(c_rows, w_ada, b_ada)


def _norm_mod(x, nw, sh, sc):
    ms = jnp.mean(x * x, axis=-1, keepdims=True)
    y = x * lax.rsqrt(ms + NORM_EPS) * nw
    return y * (1.0 + sc) + sh


def _inproj_kernel(x_ref, sh_ref, sc_ref, nw_ref, w_ref, wba_ref, p_ref, ba_ref, hb_ref, *, tn, rows):
    tm = x_ref.shape[0]
    for r0 in range(0, tm, rows):
        h = _norm_mod(x_ref[r0:r0 + rows, :], nw_ref[...], sh_ref[...], sc_ref[...])
        hb_ref[r0:r0 + rows, :] = h.astype(BF16)
    hb = hb_ref[...]
    for j in range(w_ref.shape[1] // tn):
        p_ref[:, j * tn:(j + 1) * tn] = _dot(hb, w_ref[:, j * tn:(j + 1) * tn]).astype(BF16)
    ba_ref[...] = _dot(hb, wba_ref[...])


def _inproj(x, mod3, mod_row, nw, w_main, w_ba, tm):
    b, t, d = x.shape
    nc = w_main.shape[1]
    kern = functools.partial(_inproj_kernel, tn=512, rows=128)
    return pl.pallas_call(
        kern,
        grid=(b, t // tm),
        in_specs=[pl.BlockSpec((None, tm, d), lambda i, j: (i, j, 0)),
                  pl.BlockSpec((None, 1, d), lambda i, j: (mod_row(i), 0, 0)),
                  pl.BlockSpec((None, 1, d), lambda i, j: (mod_row(i), 0, 1)),
                  pl.BlockSpec((1, d), lambda i, j: (0, 0)),
                  _resident((d, nc)),
                  _resident((d, 128))],
        out_specs=[pl.BlockSpec((None, tm, nc), lambda i, j: (i, j, 0)),
                   pl.BlockSpec((None, tm, 128), lambda i, j: (i, j, 0))],
        out_shape=[jax.ShapeDtypeStruct((b, t, nc), BF16),
                   jax.ShapeDtypeStruct((b, t, 128), F32)],
        scratch_shapes=[pltpu.VMEM((tm, d), BF16)],
        compiler_params=_cparams("arbitrary", "arbitrary"),
        name="inproj",
    )(x, mod3, mod3, nw, w_main, w_ba)


PCOL_G0 = 0
PCOL_G1 = 8
PCOL_Z = 16
PCOL_Q = 24
PCOL_K = 32
PCOL_V = 40
PCOL_F = 48
P_WIDTH = 52 * 128


def _conv_block(x_ref, w_ref, r, nblk):
    t = x_ref.shape[0]
    r0 = pl.multiple_of(r * CHUNK, CHUNK)
    main = x_ref[pl.ds(r0, CHUNK), :].astype(F32)
    p0 = pl.multiple_of(jnp.maximum(r0 - 16, 0), 16)
    n0 = pl.multiple_of(jnp.minimum(r0 + CHUNK, t - 16), 16)
    prev = x_ref[pl.ds(p0, 16), :].astype(F32)[15:16, :]
    nxt = x_ref[pl.ds(n0, 16), :].astype(F32)[0:1, :]
    prev = jnp.where(r > 0, prev, 0.0)
    nxt = jnp.where(r < nblk - 1, nxt, 0.0)
    row = lax.broadcasted_iota(jnp.int32, (CHUNK, HEAD_DIM), 0)
    dn = jnp.where(row == 0, prev, pltpu.roll(main, 1, 0))
    up = jnp.where(row == CHUNK - 1, nxt, pltpu.roll(main, CHUNK - 1, 0))
    w = w_ref[...]
    y = w[0:1, :] * dn + w[1:2, :] * main + w[2:3, :] * up
    return _silu(y)


def _l2norm(y):
    return y * lax.rsqrt(jnp.sum(y * y, axis=-1, keepdims=True) + L2_EPS)


def _deltanet_kernel(q_ref, k_ref, v_ref, z_ref, ba_ref, wq_ref, wk_ref, wv_ref, alog_ref, dtb_ref,
                     dnw_ref, s0_ref, *rest, emit_o):
    if emit_o:
        o_ref, sout_ref = rest[0], rest[1]
        scr = rest[2:]
    else:
        sout_ref = rest[0]
        scr = rest[1:]
    qn_ref, kn_ref, vn_ref, kt_ref, gh_ref, gt_ref, of_ref, ob_ref, s_ref = scr
    t = q_ref.shape[0]
    nblk = t // CHUNK
    h = pl.program_id(1)

    row_i = lax.broadcasted_iota(jnp.int32, (CHUNK, CHUNK), 0)
    col_i = lax.broadcasted_iota(jnp.int32, (CHUNK, CHUNK), 1)
    lane = col_i
    tri_lo = (row_i >= col_i).astype(BF16)
    tri_up = (row_i <= col_i).astype(BF16)
    a_neg = -jnp.exp(alog_ref[...])
    dtb = dtb_ref[...]

    def prep(r, carry):
        r0 = pl.multiple_of(r * CHUNK, CHUNK)
        q = _l2norm(_conv_block(q_ref, wq_ref, r, nblk)) * (HEAD_DIM ** -0.5)
        k = _l2norm(_conv_block(k_ref, wk_ref, r, nblk))
        v = _conv_block(v_ref, wv_ref, r, nblk)
        qn_ref[pl.ds(r0, CHUNK), :] = q
        kn_ref[pl.ds(r0, CHUNK), :] = k
        vn_ref[pl.ds(r0, CHUNK), :] = v
        kt_ref[:, pl.ds(r0, CHUNK)] = k.T
        raw = ba_ref[pl.ds(r0, CHUNK), :]
        gate = jnp.where(lane < 16, _sigmoid(raw), a_neg * _softplus(raw + dtb))
        g1, g2, g3 = _split3(gate)
        pre = _dot(tri_lo, g1) + _dot(tri_lo, g2) + _dot(tri_lo, g3)
        suf = _dot(tri_up, g1) + _dot(tri_up, g2) + _dot(tri_up, g3)
        gc = jnp.where(lane < 16, gate, jnp.where(lane < 24, pre, suf))
        gt_ref[:, pl.ds(r0, CHUNK)] = gc.T
        cols = []
        for qi, idx in enumerate((h, 8 + h, 16 + h, 24 + h)):
            colv = jnp.sum(jnp.where(lane == idx, gc, 0.0), axis=1, keepdims=True)
            cols.append(jnp.where(lane == qi, colv, 0.0))
        gh_ref[pl.ds(r0, CHUNK), :] = cols[0] + cols[1] + cols[2] + cols[3]
        return carry

    lax.fori_loop(0, nblk, prep, 0)

    s_ref[...] = s0_ref[...]
    xor = row_i ^ col_i
    lvl = jnp.zeros((CHUNK, CHUNK), jnp.int32)
    for l in range(1, LOG2_CHUNK):
        lvl = jnp.where(xor >= (1 << l), l, lvl)
    eye = (row_i == col_i).astype(F32)
    sub8 = lax.broadcasted_iota(jnp.int32, (HEADS, CHUNK), 0)

    def one_dir(d, blk):
        r0 = pl.multiple_of(blk * CHUNK, CHUNK)
        q = qn_ref[pl.ds(r0, CHUNK), :]
        k = kn_ref[pl.ds(r0, CHUNK), :]
        v = vn_ref[pl.ds(r0, CHUNK), :]
        kt = kt_ref[:, pl.ds(r0, CHUNK)]
        gh = gh_ref[pl.ds(r0, CHUNK), :]
        beta = gh[:, d:d + 1]
        gcol = gh[:, 2 + d:3 + d]
        g8 = gt_ref[16 + 8 * d:24 + 8 * d, pl.ds(r0, CHUNK)]
        grow = jnp.sum(jnp.where(sub8 == h, g8, 0.0), axis=0, keepdims=True)
        if d == 0:
            incl, strict = row_i >= col_i, row_i > col_i
            glast = gcol[CHUNK - 1:CHUNK, :]
        else:
            incl, strict = row_i <= col_i, row_i < col_i
            glast = gcol[0:1, :]
        decay = jnp.exp(jnp.where(incl, gcol - grow, -jnp.inf))
        kb = k.astype(BF16)
        kk = _dot_nt(kb, kb)
        lmat = jnp.where(strict, beta * kk * decay, 0.0)
        tinv = eye - jnp.where(lvl == 0, lmat, 0.0)
        for l in range(1, LOG2_CHUNK):
            loff = jnp.where(lvl == l, lmat, 0.0).astype(BF16)
            tb = tinv.astype(BF16)
            tinv = tinv - _dot(_dot(tb, loff).astype(BF16), tb)
        egc = jnp.exp(gcol)
        rhs = jnp.concatenate([v * beta, k * (beta * egc)], axis=1).astype(BF16)
        sol = _dot(tinv.astype(BF16), rhs)
        u, w = sol[:, :HEAD_DIM], sol[:, HEAD_DIM:]
        ktail = (kt * jnp.exp(glast - grow)).astype(BF16)
        s = s_ref[d]
        sb = s.astype(BF16)
        if emit_o:
            qh = q * egc
            r1 = _dot(jnp.concatenate([w, qh], axis=0).astype(BF16), sb)
            vnew = (u - r1[:CHUNK]).astype(BF16)
            qk = (_dot_nt(q.astype(BF16), kb) * decay).astype(BF16)
            r2 = _dot(jnp.concatenate([qk, ktail], axis=0), vnew)
            o = r1[CHUNK:] + r2[:CHUNK]
            if d == 0:
                of_ref[pl.ds(r0, CHUNK), :] = o
            else:
                ob_ref[pl.ds(r0, CHUNK), :] = o
            ds = r2[CHUNK:]
        else:
            vnew = (u - _dot(w.astype(BF16), sb)).astype(BF16)
            ds = _dot(ktail, vnew)
        s_ref[d] = s * jnp.exp(glast) + ds

    def step(i, carry):
        one_dir(0, i)
        one_dir(1, nblk - 1 - i)
        return carry

    lax.fori_loop(0, nblk, step, 0)
    sout_ref[...] = s_ref[...]

    if emit_o:
        def fin(r, carry):
            r0 = pl.multiple_of(r * CHUNK, CHUNK)
            o = of_ref[pl.ds(r0, CHUNK), :] + ob_ref[pl.ds(r0, CHUNK), :]
            on = o * lax.rsqrt(jnp.mean(o * o, axis=-1, keepdims=True) + NORM_EPS) * dnw_ref[...]
            z = z_ref[pl.ds(r0, CHUNK), :].astype(F32)
            o_ref[pl.ds(r0, CHUNK), :] = (on * _silu(z)).astype(BF16)
            return carry

        lax.fori_loop(0, nblk, fin, 0)


def _deltanet(p, ba, convw, alog_row, dtb_row, dnw, s0, emit_o):
    b, t, _ = p.shape
    tok = lambda c0: pl.BlockSpec((None, t, 128), lambda i, j: (i, 0, c0 + j))
    cw = lambda c0: pl.BlockSpec((3, 128), lambda i, j: (0, c0 + j))
    par = pl.BlockSpec((1, 128), lambda i, j: (0, 0))
    st = pl.BlockSpec((N_DIR, None, None, HEAD_DIM, HEAD_DIM), lambda i, j: (0, i, j, 0, 0))
    out_specs = [st]
    out_shape = [jax.ShapeDtypeStruct((N_DIR, b, HEADS, HEAD_DIM, HEAD_DIM), F32)]
    if emit_o:
        out_specs = [pl.BlockSpec((None, t, 128), lambda i, j: (i, 0, j))] + out_specs
        out_shape = [jax.ShapeDtypeStruct((b, t, DN_WIDTH), BF16)] + out_shape
    tc = lambda: pltpu.VMEM((t, 128), F32)
    ct = lambda: pltpu.VMEM((128, t), F32)
    return pl.pallas_call(
        functools.partial(_deltanet_kernel, emit_o=emit_o),
        grid=(b, HEADS),
        in_specs=[tok(PCOL_Q), tok(PCOL_K), tok(PCOL_V), tok(PCOL_Z),
                  pl.BlockSpec((None, t, 128), lambda i, j: (i, 0, 0)),
                  cw(0), cw(HEADS), cw(2 * HEADS), par, par, par, st],
        out_specs=out_specs,
        out_shape=out_shape,
        scratch_shapes=[tc(), tc(), tc(), ct(), tc(), ct(), tc(), tc(),
                        pltpu.VMEM((N_DIR, HEAD_DIM, HEAD_DIM), F32)],
        compiler_params=_cparams("arbitrary", "arbitrary"),
        name="deltanet_x" if emit_o else "deltanet_ctx",
    )(p, p, p, p, ba, convw, convw, convw, alog_row, dtb_row, dnw, s0)


def _fourier_kernel(x_ref, wc_ref, twc_ref, tws_ref, cn_ref, sn_ref, o_ref, y_ref, a_ref, *, rows):
    n1 = FFT_N1
    t = x_ref.shape[0]
    n2 = t // n1
    fw = F_WIDTH
    for t1 in range(n1):
        y_ref[t1] = _dot(x_ref[t1 * n2:(t1 + 1) * n2, :], wc_ref[...])

    ang = [2.0 * math.pi * m / n1 for m in range(n1)]
    cs = [(round(math.cos(a), 12), round(math.sin(a), 12)) for a in ang]

    def axpy(acc, coef, val):
        if coef == 0.0:
            return acc
        if acc is None:
            return val if coef == 1.0 else (-val if coef == -1.0 else coef * val)
        if coef == 1.0:
            return acc + val
        if coef == -1.0:
            return acc - val
        return acc + coef * val

    hw = 256

    def stage1(rb, carry):
        r0 = pl.multiple_of(rb * rows, rows)
        for l0 in range(0, fw, hw):
            yr = [y_ref[t1, pl.ds(r0, rows), l0:l0 + hw] for t1 in range(n1)]
            yi = [y_ref[t1, pl.ds(r0, rows), fw + l0:fw + l0 + hw] for t1 in range(n1)]
            for k1 in range(n1):
                ar = None
                ai = None
                for t1 in range(n1):
                    c, s = cs[(t1 * k1) % n1]
                    ar = axpy(axpy(ar, c, yr[t1]), s, yi[t1])
                    ai = axpy(axpy(ai, c, yi[t1]), -s, yr[t1])
                twc = jnp.concatenate([twc_ref[k1, pl.ds(r0, rows), :]] * (hw // 128), axis=1)
                tws = jnp.concatenate([tws_ref[k1, pl.ds(r0, rows), :]] * (hw // 128), axis=1)
                a_ref[k1, 0, pl.ds(r0, rows), l0:l0 + hw] = (ar * twc + ai * tws).astype(BF16)
                a_ref[k1, 1, pl.ds(r0, rows), l0:l0 + hw] = (ai * twc - ar * tws).astype(BF16)
        return carry

    lax.fori_loop(0, n2 // rows, stage1, 0)

    for k1 in range(n1):
        zr = _dot(cn_ref[...], a_ref[k1, 0]) + _dot(sn_ref[...], a_ref[k1, 1])
        for g in range(N_GROUPS):
            o_ref[g, pl.ds(k1, n2, stride=n1), :] = zr[:, g * GROUP_DIM:(g + 1) * GROUP_DIM]


def _fourier_consts(t):
    n1 = FFT_N1
    n2 = t // n1
    scale = 1.0 / math.sqrt(t * GROUP_DIM)
    j = np.arange(GROUP_DIM)
    ang = 2.0 * np.pi * np.outer(j, j) / GROUP_DIM
    wc = np.zeros((F_WIDTH, 2 * F_WIDTH), np.float32)
    for g in range(N_GROUPS):
        sl = slice(g * GROUP_DIM, (g + 1) * GROUP_DIM)
        wc[sl, sl] = np.cos(ang) * scale
        wc[sl, F_WIDTH + g * GROUP_DIM:F_WIDTH + (g + 1) * GROUP_DIM] = -np.sin(ang) * scale
    t2 = np.arange(n2)
    k1 = np.arange(n1)
    tw = 2.0 * np.pi * np.outer(k1, t2) / t
    a2 = 2.0 * np.pi * np.outer(t2, t2) / n2
    return (wc, np.cos(tw).astype(np.float32), np.sin(tw).astype(np.float32),
            np.cos(a2).astype(np.float32), np.sin(a2).astype(np.float32))


def _fourier(p):
    b, t, _ = p.shape
    n1 = FFT_N1
    n2 = t // n1
    wc, twc, tws, cn, sn = _fourier_consts(t)
    wc = jnp.asarray(wc, BF16)
    twc = jnp.broadcast_to(jnp.asarray(twc)[:, :, None], (n1, n2, 128))
    tws = jnp.broadcast_to(jnp.asarray(tws)[:, :, None], (n1, n2, 128))
    cn = jnp.asarray(cn, BF16)
    sn = jnp.asarray(sn, BF16)
    return pl.pallas_call(
        functools.partial(_fourier_kernel, rows=16),
        grid=(b,),
        in_specs=[pl.BlockSpec((None, t, F_WIDTH), lambda i: (i, 0, PCOL_F // 4)),
                  _resident((F_WIDTH, 2 * F_WIDTH)),
                  _resident((n1, n2, 128)), _resident((n1, n2, 128)),
                  _resident((n2, n2)), _resident((n2, n2))],
        out_specs=pl.BlockSpec((None, N_GROUPS, t, GROUP_DIM), lambda i: (i, 0, 0, 0)),
        out_shape=jax.ShapeDtypeStruct((b, N_GROUPS, t, GROUP_DIM), F32),
        scratch_shapes=[pltpu.VMEM((n1, n2, 2 * F_WIDTH), F32),
                        pltpu.VMEM((n1, 2, n2, F_WIDTH), BF16)],
        compiler_params=_cparams("arbitrary"),
        name="fourier",
    )(p, wc, twc, tws, cn, sn)


def _merge_kernel(x_ref, g0_ref, g1_ref, o_ref, f_ref, gt_ref, nw_ref, wf_ref, wd_ref, wo_ref, out_ref,
                  *, rows):
    tm = x_ref.shape[0]
    for r0 in range(0, tm, rows):
        sl = slice(r0, r0 + rows)
        fo = jnp.concatenate([f_ref[g, sl, :] for g in range(N_GROUPS)], axis=1).astype(BF16)
        yf = _dot(fo, wf_ref[...])
        yd = _dot(o_ref[sl, :], wd_ref[...])
        m = _sigmoid(g0_ref[sl, :].astype(F32)) * yf + _sigmoid(g1_ref[sl, :].astype(F32)) * yd
        y = _dot(m.astype(BF16), wo_ref[...])
        yn = y * lax.rsqrt(jnp.mean(y * y, axis=-1, keepdims=True) + NORM_EPS) * nw_ref[...]
        out_ref[sl, :] = x_ref[sl, :] + gt_ref[...] * yn


def _merge(x, p, o, four, mod3, nw, wf, wd, wo, tm):
    b, t, d = x.shape
    return pl.pallas_call(
        functools.partial(_merge_kernel, rows=256),
        grid=(b, t // tm),
        in_specs=[pl.BlockSpec((None, tm, d), lambda i, j: (i, j, 0)),
                  pl.BlockSpec((None, tm, d), lambda i, j: (i, j, PCOL_G0 // 8)),
                  pl.BlockSpec((None, tm, d), lambda i, j: (i, j, PCOL_G1 // 8)),
                  pl.BlockSpec((None, tm, DN_WIDTH), lambda i, j: (i, j, 0)),
                  pl.BlockSpec((None, N_GROUPS, tm, GROUP_DIM), lambda i, j: (i, 0, j, 0)),
                  pl.BlockSpec((None, 1, d), lambda i, j: (i, 0, 2)),
                  pl.BlockSpec((1, d), lambda i, j: (0, 0)),
                  _resident(wf.shape), _resident(wd.shape), _resident(wo.shape)],
        out_specs=pl.BlockSpec((None, tm, d), lambda i, j: (i, j, 0)),
        out_shape=jax.ShapeDtypeStruct((b, t, d), F32),
        compiler_params=_cparams("arbitrary", "arbitrary"),
        name="merge",
    )(x, p, p, o, four, mod3, nw, wf, wd, wo)


def _ffn_kernel(x_ref, xp_ref, xn_ref, sh_ref, sc_ref, gt_ref, npre_ref, npost_ref,
                wa_ref, wu_ref, cw_ref, wd_ref, out_ref, hb_ref, a_ref, acc_ref):
    tm = x_ref.shape[0]
    gw = GRID_W
    i = pl.program_id(1)
    last = pl.num_programs(1) - 1
    nf = wa_ref.shape[0]

    def nm(xt):
        return _norm_mod(xt, npre_ref[...], sh_ref[...], sc_ref[...]).astype(BF16)

    hb_ref[0:gw, :] = nm(xp_ref[...])
    for r0 in range(0, tm, 128):
        hb_ref[gw + r0:gw + r0 + 128, :] = nm(x_ref[r0:r0 + 128, :])
    hb_ref[gw + tm:, :] = nm(xn_ref[...])
    acc_ref[...] = jnp.zeros_like(acc_ref)
    top = jnp.where(i > 0, 1.0, 0.0)
    bot = jnp.where(i < last, 1.0, 0.0)
    colid = lax.broadcasted_iota(jnp.int32, (tm, 1), 0) % gw
    has_l = colid != 0
    has_r = colid != gw - 1

    def body(fj, carry):
        a = _dot(hb_ref[...], wa_ref[fj])
        a_ref[0:gw, :] = a[0:gw] * top
        a_ref[gw:gw + tm, :] = a[gw:gw + tm]
        a_ref[gw + tm:, :] = a[gw + tm:] * bot
        cw = cw_ref[fj]
        s0 = a_ref[0:tm, :]
        s1 = a_ref[gw:gw + tm, :]
        s2 = a_ref[2 * gw:2 * gw + tm, :]
        left = cw[0:1] * s0 + cw[3:4] * s1 + cw[6:7] * s2
        mid = cw[1:2] * s0 + cw[4:5] * s1 + cw[7:8] * s2
        right = cw[2:3] * s0 + cw[5:6] * s1 + cw[8:9] * s2
        conv = (mid + jnp.where(has_l, pltpu.roll(left, 1, 0), 0.0)
                + jnp.where(has_r, pltpu.roll(right, tm - 1, 0), 0.0))
        u = _dot(hb_ref[gw:gw + tm, :], wu_ref[fj])
        act = (_silu(conv) * u).astype(BF16)
        acc_ref[...] += _dot(act, wd_ref[fj])
        return carry

    lax.fori_loop(0, nf, body, 0)
    for r0 in range(0, tm, 128):
        sl = slice(r0, r0 + 128)
        y = acc_ref[sl, :]
        yn = y * lax.rsqrt(jnp.mean(y * y, axis=-1, keepdims=True) + NORM_EPS) * npost_ref[...]
        out_ref[sl, :] = x_ref[sl, :] + gt_ref[...] * yn


def _ffn(x, mod3, npre, npost, wa, wu, cw, wd, tm):
    b, t, d = x.shape
    gw = GRID_W
    rpt = tm // gw
    nrow = t // gw
    nf, _, tf = wa.shape
    return pl.pallas_call(
        _ffn_kernel,
        grid=(b, t // tm),
        in_specs=[pl.BlockSpec((None, tm, d), lambda i, j: (i, j, 0)),
                  pl.BlockSpec((None, gw, d), lambda i, j: (i, jnp.maximum(j * rpt - 1, 0), 0)),
                  pl.BlockSpec((None, gw, d), lambda i, j: (i, jnp.minimum(j * rpt + rpt, nrow - 1), 0)),
                  pl.BlockSpec((None, 1, d), lambda i, j: (i, 0, 3)),
                  pl.BlockSpec((None, 1, d), lambda i, j: (i, 0, 4)),
                  pl.BlockSpec((None, 1, d), lambda i, j: (i, 0, 5)),
                  pl.BlockSpec((1, d), lambda i, j: (0, 0)),
                  pl.BlockSpec((1, d), lambda i, j: (0, 0)),
                  _resident(wa.shape), _resident(wu.shape), _resident(cw.shape), _resident(wd.shape)],
        out_specs=pl.BlockSpec((None, tm, d), lambda i, j: (i, j, 0)),
        out_shape=jax.ShapeDtypeStruct((b, t, d), F32),
        scratch_shapes=[pltpu.VMEM((tm + 2 * gw, d), BF16),
                        pltpu.VMEM((tm + 2 * gw, tf), F32),
                        pltpu.VMEM((tm, d), F32)],
        compiler_params=_cparams("arbitrary", "arbitrary"),
        name="ffn",
    )(x, x, x, mod3, mod3, mod3, npre, npost, wa, wu, cw, wd)


def kernel(x, c, ctx, c_ctx, w_ada, b_ada, norm_pre_mix, norm_post_mix, norm_pre_ffn, norm_post_ffn,
           w_in, conv_qkv, a_log, dt_bias, dn_norm, w_fourier, w_dn, w_out, w_up, conv_ffn, w_down):
    assert w_ada.shape[0] == 1, "single-layer stack"
    bsz, seq, d = x.shape
    d_ff = w_down.shape[1]
    assert d == DN_WIDTH and seq % (FFT_N1 * 16) == 0 and seq % 512 == 0 and ctx.shape[1] % CHUNK == 0

    pad = (-(bsz + 1)) % 8
    c_rows = jnp.concatenate([c, c_ctx[None, :], jnp.zeros((pad, d), F32)], axis=0)
    mod = _ada(c_rows, w_ada[0], b_ada)
    mod3 = mod.reshape(mod.shape[0], 1, 6 * d)

    wi = w_in[0]
    o_q = F_WIDTH
    o_z = o_q + 3 * DN_WIDTH
    o_b = o_z + DN_WIDTH
    o_g = o_b + 2 * N_DIR * HEADS
    w_main = jnp.concatenate([wi[:, o_g:], wi[:, o_z:o_b], wi[:, o_q:o_z], wi[:, :o_q]], axis=1).astype(BF16)
    w_ba = jnp.pad(wi[:, o_b:o_g], ((0, 0), (0, 128 - 2 * N_DIR * HEADS))).astype(BF16)
    npm = norm_pre_mix
    p_c, ba_c = _inproj(ctx, mod3, lambda i: bsz, npm, w_main, w_ba, tm=ctx.shape[1])
    p_x, ba_x = _inproj(x, mod3, lambda i: i, npm, w_main, w_ba, tm=512)

    lanes = jnp.zeros((1, 128), F32)
    alog_row = lanes.at[0, 16:32].set(a_log[0].reshape(-1))
    dtb_row = lanes.at[0, 16:32].set(dt_bias[0].reshape(-1))
    zero = jnp.zeros((N_DIR, bsz, HEADS, HEAD_DIM, HEAD_DIM), F32)
    (s_ctx,) = _deltanet(p_c, ba_c, conv_qkv[0], alog_row, dtb_row, dn_norm, zero, emit_o=False)
    o_x, _ = _deltanet(p_x, ba_x, conv_qkv[0], alog_row, dtb_row, dn_norm, s_ctx, emit_o=True)

    four = _fourier(p_x)
    x1 = _merge(x, p_x, o_x, four, mod3, norm_post_mix, w_fourier[0].astype(BF16),
                w_dn[0].astype(BF16), w_out[0].astype(BF16), tm=512)

    tf = 256
    nf = d_ff // tf
    wup = w_up[0].astype(BF16)
    wa = wup[:, :d_ff].reshape(d, nf, tf).transpose(1, 0, 2)
    wu = wup[:, d_ff:].reshape(d, nf, tf).transpose(1, 0, 2)
    cw = conv_ffn[0].reshape(9, nf, tf).transpose(1, 0, 2)
    wd = w_down[0].astype(BF16).reshape(nf, tf, d)
    return _ffn(x1, mod3, norm_pre_ffn, norm_post_ffn, wa, wu, cw, wd, tm=512)
```

```python
import functools
import math

import numpy as np
import jax
import jax.numpy as jnp
from jax import lax
from jax.experimental import pallas as pl
from jax.experimental.pallas import tpu as pltpu

F32 = jnp.float32
BF16 = jnp.bfloat16

GRID_W = 64
N_GROUPS = 4
GROUP_DIM = 128
F_WIDTH = N_GROUPS * GROUP_DIM
HEADS = 8
HEAD_DIM = 128
DN_WIDTH = HEADS * HEAD_DIM
N_DIR = 2
NORM_EPS = 1e-6
L2_EPS = 1e-6
CHUNK = 128
LOG2_CHUNK = 7
FFT_N1 = 8
VMEM_LIMIT = 56 * 1024 * 1024


def _cparams(*sem):
    return pltpu.CompilerParams(dimension_semantics=sem, vmem_limit_bytes=VMEM_LIMIT)


def _sigmoid(x):
    return 1.0 / (1.0 + jnp.exp(-x))


def _silu(x):
    return x * _sigmoid(x)


def _softplus(x):
    return jnp.maximum(x, 0.0) + jnp.log1p(jnp.exp(-jnp.abs(x)))


def _dot(a, b):
    return jnp.dot(a, b, preferred_element_type=F32)


def _dot_nt(a, b):
    return lax.dot_general(a, b, (((1,), (1,)), ((), ())), preferred_element_type=F32)


def _split3(x):
    x1 = x.astype(BF16)
    r1 = x - x1.astype(F32)
    x2 = r1.astype(BF16)
    x3 = (r1 - x2.astype(F32)).astype(BF16)
    return x1, x2, x3


def _resident(shape):
    nd = len(shape)
    return pl.BlockSpec(shape, lambda *_: (0,) * nd, pipeline_mode=pl.Buffered(1))


def _ada_kernel(c_ref, w_ref, b_ref, o_ref):
    s = _silu(c_ref[...])
    o_ref[...] = jnp.dot(s, w_ref[...], preferred_element_type=F32,
                         precision=lax.Precision.HIGHEST) + b_ref[...]


def _ada(c_rows, w_ada, b_ada):
    rows, d = c_rows.shape
    n = w_ada.shape[1]
    tn = n // 4
    return pl.pallas_call(
        _ada_kernel,
        grid=(n // tn,),
        in_specs=[pl.BlockSpec((rows, d), lambda j: (0, 0)),
                  pl.BlockSpec((d, tn), lambda j: (0, j)),
                  pl.BlockSpec((1, tn), lambda j: (0, j))],
        out_specs=pl.BlockSpec((rows, tn), lambda j: (0, j)),
        out_shape=jax.ShapeDtypeStruct((rows, n), F32),
        compiler_params=_cparams("arbitrary"),
        name="ada",
    )(c_rows, w_ada, b_ada)


def _norm_mod(x, nw, sh, sc):
    ms = jnp.mean(x * x, axis=-1, keepdims=True)
    y = x * lax.rsqrt(ms + NORM_EPS) * nw
    return y * (1.0 + sc) + sh


def _inproj_kernel(x_ref, sh_ref, sc_ref, nw_ref, w_ref, wba_ref, p_ref, ba_ref, hb_ref, *, tn, rows):
    tm = x_ref.shape[0]
    for r0 in range(0, tm, rows):
        h = _norm_mod(x_ref[r0:r0 + rows, :], nw_ref[...], sh_ref[...], sc_ref[...])
        hb_ref[r0:r0 + rows, :] = h.astype(BF16)
    hb = hb_ref[...]
    for j in range(w_ref.shape[1] // tn):
        p_ref[:, j * tn:(j + 1) * tn] = _dot(hb, w_ref[:, j * tn:(j + 1) * tn]).astype(BF16)
    ba_ref[...] = _dot(hb, wba_ref[...])


def _inproj(x, mod3, mod_row, nw, w_main, w_ba, tm):
    b, t, d = x.shape
    nc = w_main.shape[1]
    kern = functools.partial(_inproj_kernel, tn=512, rows=128)
    return pl.pallas_call(
        kern,
        grid=(b, t // tm),
        in_specs=[pl.BlockSpec((None, tm, d), lambda i, j: (i, j, 0)),
                  pl.BlockSpec((None, 1, d), lambda i, j: (mod_row(i), 0, 0)),
                  pl.BlockSpec((None, 1, d), lambda i, j: (mod_row(i), 0, 1)),
                  pl.BlockSpec((1, d), lambda i, j: (0, 0)),
                  _resident((d, nc)),
                  _resident((d, 128))],
        out_specs=[pl.BlockSpec((None, tm, nc), lambda i, j: (i, j, 0)),
                   pl.BlockSpec((None, tm, 128), lambda i, j: (i, j, 0))],
        out_shape=[jax.ShapeDtypeStruct((b, t, nc), BF16),
                   jax.ShapeDtypeStruct((b, t, 128), F32)],
        scratch_shapes=[pltpu.VMEM((tm, d), BF16)],
        compiler_params=_cparams("arbitrary", "arbitrary"),
        name="inproj",
    )(x, mod3, mod3, nw, w_main, w_ba)


PCOL_G0 = 0
PCOL_G1 = 8
PCOL_Z = 16
PCOL_Q = 24
PCOL_K = 32
PCOL_V = 40
PCOL_F = 48
P_WIDTH = 52 * 128


def _conv_block(x_ref, w_ref, r, nblk):
    t = x_ref.shape[0]
    r0 = pl.multiple_of(r * CHUNK, CHUNK)
    main = x_ref[pl.ds(r0, CHUNK), :].astype(F32)
    p0 = pl.multiple_of(jnp.maximum(r0 - 16, 0), 16)
    n0 = pl.multiple_of(jnp.minimum(r0 + CHUNK, t - 16), 16)
    prev = x_ref[pl.ds(p0, 16), :].astype(F32)[15:16, :]
    nxt = x_ref[pl.ds(n0, 16), :].astype(F32)[0:1, :]
    prev = jnp.where(r > 0, prev, 0.0)
    nxt = jnp.where(r < nblk - 1, nxt, 0.0)
    row = lax.broadcasted_iota(jnp.int32, (CHUNK, HEAD_DIM), 0)
    dn = jnp.where(row == 0, prev, pltpu.roll(main, 1, 0))
    up = jnp.where(row == CHUNK - 1, nxt, pltpu.roll(main, CHUNK - 1, 0))
    w = w_ref[...]
    y = w[0:1, :] * dn + w[1:2, :] * main + w[2:3, :] * up
    return _silu(y)


def _l2norm(y):
    return y * lax.rsqrt(jnp.sum(y * y, axis=-1, keepdims=True) + L2_EPS)


def _deltanet_kernel(q_ref, k_ref, v_ref, z_ref, ba_ref, wq_ref, wk_ref, wv_ref, alog_ref, dtb_ref,
                     dnw_ref, s0_ref, *rest, emit_o, unroll):
    if emit_o:
        o_ref, sout_ref = rest[0], rest[1]
        mq_ref, n_ref, oacc_ref, s_ref = rest[2:]
    else:
        sout_ref = rest[0]
        mq_ref, n_ref, s_ref = rest[1:]
    t = q_ref.shape[0]
    nblk = t // CHUNK
    h = pl.program_id(1)

    row_i = lax.broadcasted_iota(jnp.int32, (CHUNK, CHUNK), 0)
    col_i = lax.broadcasted_iota(jnp.int32, (CHUNK, CHUNK), 1)
    lane = col_i
    tri_lo = (row_i >= col_i).astype(BF16)
    tri_up = (row_i <= col_i).astype(BF16)
    a_neg = -jnp.exp(alog_ref[...])
    dtb = dtb_ref[...]
    xor = row_i ^ col_i
    lvl = jnp.zeros((CHUNK, CHUNK), jnp.int32)
    for l in range(1, LOG2_CHUNK):
        lvl = jnp.where(xor >= (1 << l), l, lvl)
    eye = (row_i == col_i).astype(F32)
    sub8 = lax.broadcasted_iota(jnp.int32, (HEADS, CHUNK), 0)

    def block_terms(blk):
        r0 = pl.multiple_of(blk * CHUNK, CHUNK)
        q = _l2norm(_conv_block(q_ref, wq_ref, blk, nblk)) * (HEAD_DIM ** -0.5)
        k = _l2norm(_conv_block(k_ref, wk_ref, blk, nblk))
        v = _conv_block(v_ref, wv_ref, blk, nblk)
        raw = ba_ref[pl.ds(r0, CHUNK), :]
        gate = jnp.where(lane < 16, _sigmoid(raw), a_neg * _softplus(raw + dtb))
        return dict(blk=blk, r0=r0, q=q, k=k, v=v, kt=k.T, kb=k.astype(BF16), gate=gate)

    def chain_setup(bt, d):
        gc, gct, kk = bt["gc"], bt["gct"], bt["kk"]
        beta = jnp.sum(jnp.where(lane == 8 * d + h, gc, 0.0), axis=1, keepdims=True)
        gcol = jnp.sum(jnp.where(lane == 16 + 8 * d + h, gc, 0.0), axis=1, keepdims=True)
        grow = jnp.sum(jnp.where(sub8 == h, gct[16 + 8 * d:24 + 8 * d, :], 0.0),
                       axis=0, keepdims=True)
        if d == 0:
            incl, strict = row_i >= col_i, row_i > col_i
            glast = gcol[CHUNK - 1:CHUNK, :]
        else:
            incl, strict = row_i <= col_i, row_i < col_i
            glast = gcol[0:1, :]
        decay = jnp.exp(jnp.where(incl, gcol - grow, -jnp.inf))
        lmat = jnp.where(strict, beta * kk * decay, 0.0)
        egc = jnp.exp(gcol)
        rhs = jnp.concatenate([bt["k"] * (beta * egc), bt["v"] * beta], axis=1).astype(BF16)
        ktail = (bt["kt"] * jnp.exp(glast - grow)).astype(BF16)
        return dict(bt=bt, d=d, decay=decay, lmat=lmat, egc=egc, rhs=rhs, ktail=ktail, glast=glast,
                    tinv=eye - jnp.where(lvl == 0, lmat, 0.0))

    def pass_a(i, carry):
        blocks = [block_terms(i * unroll + j) for j in range(unroll)]
        for bt in blocks:
            bt["g3"] = _split3(bt["gate"])
        for bt in blocks:
            bt["pre"] = sum(_dot(tri_lo, g) for g in bt["g3"])
        for bt in blocks:
            bt["suf"] = sum(_dot(tri_up, g) for g in bt["g3"])
        for bt in blocks:
            bt["kk"] = _dot_nt(bt["kb"], bt["kb"])
            if emit_o:
                bt["qkt"] = _dot_nt(bt["q"].astype(BF16), bt["kb"])
        for bt in blocks:
            bt["gc"] = jnp.where(lane < 16, bt["gate"], jnp.where(lane < 24, bt["pre"], bt["suf"]))
            bt["gct"] = bt["gc"].T
        chains = [chain_setup(bt, d) for bt in blocks for d in range(N_DIR)]
        for l in range(1, LOG2_CHUNK):
            for c in chains:
                c["tb"] = c["tinv"].astype(BF16)
                c["x"] = _dot(c["tb"], jnp.where(lvl == l, c["lmat"], 0.0).astype(BF16)).astype(BF16)
            for c in chains:
                c["tinv"] = c["tinv"] - _dot(c["x"], c["tb"])
        for c in chains:
            c["wu"] = _dot(c["tinv"].astype(BF16), c["rhs"]).astype(BF16)
        for c in chains:
            c["ku"] = _dot(c["ktail"], c["wu"])
            if emit_o:
                c["qu"] = _dot((c["bt"]["qkt"] * c["decay"]).astype(BF16), c["wu"])
        for c in chains:
            d, bt = c["d"], c["bt"]
            m = eye * jnp.exp(c["glast"]) - c["ku"][:, :HEAD_DIM]
            n_ref[d, bt["blk"]] = c["ku"][:, HEAD_DIM:]
            if emit_o:
                qe = bt["q"] * c["egc"] - c["qu"][:, :HEAD_DIM]
                mq_ref[d, bt["blk"]] = jnp.concatenate([m, qe], axis=0).astype(BF16)
                oacc_ref[d, pl.ds(bt["r0"], CHUNK), :] = c["qu"][:, HEAD_DIM:]
            else:
                mq_ref[d, bt["blk"]] = m.astype(BF16)
        return carry

    lax.fori_loop(0, nblk // unroll, pass_a, 0)

    s_ref[...] = s0_ref[...]

    def pass_b(i, carry):
        for d in range(N_DIR):
            blk = i if d == 0 else nblk - 1 - i
            r = _dot(mq_ref[d, blk], s_ref[d].astype(BF16))
            if emit_o:
                r0 = pl.multiple_of(blk * CHUNK, CHUNK)
                oacc_ref[d, pl.ds(r0, CHUNK), :] += r[CHUNK:]
            s_ref[d] = r[:CHUNK] + n_ref[d, blk]
        return carry

    lax.fori_loop(0, nblk, pass_b, 0)
    sout_ref[...] = s_ref[...]

    if emit_o:
        def fin(r, carry):
            r0 = pl.multiple_of(r * CHUNK, CHUNK)
            o = oacc_ref[0, pl.ds(r0, CHUNK), :] + oacc_ref[1, pl.ds(r0, CHUNK), :]
            on = o * lax.rsqrt(jnp.mean(o * o, axis=-1, keepdims=True) + NORM_EPS) * dnw_ref[...]
            z = z_ref[pl.ds(r0, CHUNK), :].astype(F32)
            o_ref[pl.ds(r0, CHUNK), :] = (on * _silu(z)).astype(BF16)
            return carry

        lax.fori_loop(0, nblk, fin, 0)


def _deltanet(p, ba, convw, alog_row, dtb_row, dnw, s0, emit_o):
    b, t, _ = p.shape
    nblk = t // CHUNK
    unroll = 4 if nblk % 4 == 0 else (2 if nblk % 2 == 0 else 1)
    tok = lambda c0: pl.BlockSpec((None, t, 128), lambda i, j: (i, 0, c0 + j))
    cw = lambda c0: pl.BlockSpec((3, 128), lambda i, j: (0, c0 + j))
    par = pl.BlockSpec((1, 128), lambda i, j: (0, 0))
    st = pl.BlockSpec((N_DIR, None, None, HEAD_DIM, HEAD_DIM), lambda i, j: (0, i, j, 0, 0))
    out_specs = [st]
    out_shape = [jax.ShapeDtypeStruct((N_DIR, b, HEADS, HEAD_DIM, HEAD_DIM), F32)]
    scratch = [pltpu.VMEM((N_DIR, nblk, (2 if emit_o else 1) * CHUNK, HEAD_DIM), BF16),
               pltpu.VMEM((N_DIR, nblk, HEAD_DIM, HEAD_DIM), F32)]
    if emit_o:
        out_specs = [pl.BlockSpec((None, t, 128), lambda i, j: (i, 0, j))] + out_specs
        out_shape = [jax.ShapeDtypeStruct((b, t, DN_WIDTH), BF16)] + out_shape
        scratch.append(pltpu.VMEM((N_DIR, t, HEAD_DIM), F32))
    scratch.append(pltpu.VMEM((N_DIR, HEAD_DIM, HEAD_DIM), F32))
    return pl.pallas_call(
        functools.partial(_deltanet_kernel, emit_o=emit_o, unroll=unroll),
        grid=(b, HEADS),
        in_specs=[tok(PCOL_Q), tok(PCOL_K), tok(PCOL_V), tok(PCOL_Z),
                  pl.BlockSpec((None, t, 128), lambda i, j: (i, 0, 0)),
                  cw(0), cw(HEADS), cw(2 * HEADS), par, par, par, st],
        out_specs=out_specs,
        out_shape=out_shape,
        scratch_shapes=scratch,
        compiler_params=_cparams("arbitrary", "arbitrary"),
        name="deltanet_x" if emit_o else "deltanet_ctx",
    )(p, p, p, p, ba, convw, convw, convw, alog_row, dtb_row, dnw, s0)


def _fourier_kernel(x_ref, wc_ref, twc_ref, tws_ref, cn_ref, sn_ref, o_ref, y_ref, a_ref, *, rows):
    n1 = FFT_N1
    t = x_ref.shape[0]
    n2 = t // n1
    fw = F_WIDTH
    wc = wc_ref[...].astype(BF16)
    for t1 in range(n1):
        y_ref[t1] = _dot(x_ref[t1 * n2:(t1 + 1) * n2, :], wc).astype(BF16)

    ang = [2.0 * math.pi * m / n1 for m in range(n1)]
    snap = lambda val: float(round(val)) if abs(val - round(val)) < 1e-9 else val
    cs = [(snap(math.cos(a)), snap(math.sin(a))) for a in ang]

    def axpy(acc, coef, val):
        if coef == 0.0:
            return acc
        if acc is None:
            return val if coef == 1.0 else (-val if coef == -1.0 else coef * val)
        if coef == 1.0:
            return acc + val
        if coef == -1.0:
            return acc - val
        return acc + coef * val

    hw = 256

    def stage1(rb, carry):
        r0 = pl.multiple_of(rb * rows, rows)
        for l0 in range(0, fw, hw):
            yr = [y_ref[t1, pl.ds(r0, rows), l0:l0 + hw].astype(F32) for t1 in range(n1)]
            yi = [y_ref[t1, pl.ds(r0, rows), fw + l0:fw + l0 + hw].astype(F32) for t1 in range(n1)]
            for k1 in range(n1):
                ar = None
                ai = None
                for t1 in range(n1):
                    c, s = cs[(t1 * k1) % n1]
                    ar = axpy(axpy(ar, c, yr[t1]), s, yi[t1])
                    ai = axpy(axpy(ai, c, yi[t1]), -s, yr[t1])
                twc = jnp.concatenate([twc_ref[k1, pl.ds(r0, rows), :]] * (hw // 128), axis=1)
                tws = jnp.concatenate([tws_ref[k1, pl.ds(r0, rows), :]] * (hw // 128), axis=1)
                a_ref[k1, 0, pl.ds(r0, rows), l0:l0 + hw] = (ar * twc + ai * tws).astype(BF16)
                a_ref[k1, 1, pl.ds(r0, rows), l0:l0 + hw] = (ai * twc - ar * tws).astype(BF16)
        return carry

    lax.fori_loop(0, n2 // rows, stage1, 0)

    cn = cn_ref[...].astype(BF16)
    sn = sn_ref[...].astype(BF16)
    for k1 in range(n1):
        zr = _dot(cn, a_ref[k1, 0]) + _dot(sn, a_ref[k1, 1])
        for g in range(N_GROUPS):
            o_ref[g, pl.ds(k1, n2, stride=n1), :] = zr[:, g * GROUP_DIM:(g + 1) * GROUP_DIM]


def _fourier_consts(t):
    n1 = FFT_N1
    n2 = t // n1
    scale = 1.0 / math.sqrt(t * GROUP_DIM)
    j = np.arange(GROUP_DIM)
    ang = 2.0 * np.pi * np.outer(j, j) / GROUP_DIM
    wc = np.zeros((F_WIDTH, 2 * F_WIDTH), np.float32)
    for g in range(N_GROUPS):
        sl = slice(g * GROUP_DIM, (g + 1) * GROUP_DIM)
        wc[sl, sl] = np.cos(ang) * scale
        wc[sl, F_WIDTH + g * GROUP_DIM:F_WIDTH + (g + 1) * GROUP_DIM] = -np.sin(ang) * scale
    t2 = np.arange(n2)
    k1 = np.arange(n1)
    tw = 2.0 * np.pi * np.outer(k1, t2) / t
    a2 = 2.0 * np.pi * np.outer(t2, t2) / n2
    return (wc, np.cos(tw).astype(np.float32), np.sin(tw).astype(np.float32),
            np.cos(a2).astype(np.float32), np.sin(a2).astype(np.float32))


def _fourier(p):
    b, t, _ = p.shape
    n1 = FFT_N1
    n2 = t // n1
    wc, twc, tws, cn, sn = _fourier_consts(t)
    wc, cn, sn = jnp.asarray(wc), jnp.asarray(cn), jnp.asarray(sn)
    twc = jnp.broadcast_to(jnp.asarray(twc)[:, :, None], (n1, n2, 128))
    tws = jnp.broadcast_to(jnp.asarray(tws)[:, :, None], (n1, n2, 128))
    return pl.pallas_call(
        functools.partial(_fourier_kernel, rows=16),
        grid=(b,),
        in_specs=[pl.BlockSpec((None, t, F_WIDTH), lambda i: (i, 0, PCOL_F // 4)),
                  _resident((F_WIDTH, 2 * F_WIDTH)),
                  _resident((n1, n2, 128)), _resident((n1, n2, 128)),
                  _resident((n2, n2)), _resident((n2, n2))],
        out_specs=pl.BlockSpec((None, N_GROUPS, t, GROUP_DIM), lambda i: (i, 0, 0, 0)),
        out_shape=jax.ShapeDtypeStruct((b, N_GROUPS, t, GROUP_DIM), F32),
        scratch_shapes=[pltpu.VMEM((n1, n2, 2 * F_WIDTH), BF16),
                        pltpu.VMEM((n1, 2, n2, F_WIDTH), BF16)],
        compiler_params=_cparams("arbitrary"),
        name="fourier",
    )(p, wc, twc, tws, cn, sn)


def _merge_kernel(x_ref, g0_ref, g1_ref, o_ref, f_ref, gt_ref, nw_ref, wf_ref, wd_ref, wo_ref, out_ref,
                  *, rows):
    tm = x_ref.shape[0]
    for r0 in range(0, tm, rows):
        sl = slice(r0, r0 + rows)
        fo = jnp.concatenate([f_ref[g, sl, :] for g in range(N_GROUPS)], axis=1).astype(BF16)
        yf = _dot(fo, wf_ref[...])
        yd = _dot(o_ref[sl, :], wd_ref[...])
        m = _sigmoid(g0_ref[sl, :].astype(F32)) * yf + _sigmoid(g1_ref[sl, :].astype(F32)) * yd
        y = _dot(m.astype(BF16), wo_ref[...])
        yn = y * lax.rsqrt(jnp.mean(y * y, axis=-1, keepdims=True) + NORM_EPS) * nw_ref[...]
        out_ref[sl, :] = x_ref[sl, :] + gt_ref[...] * yn


def _merge(x, p, o, four, mod3, nw, wf, wd, wo, tm):
    b, t, d = x.shape
    return pl.pallas_call(
        functools.partial(_merge_kernel, rows=256),
        grid=(b, t // tm),
        in_specs=[pl.BlockSpec((None, tm, d), lambda i, j: (i, j, 0)),
                  pl.BlockSpec((None, tm, d), lambda i, j: (i, j, PCOL_G0 // 8)),
                  pl.BlockSpec((None, tm, d), lambda i, j: (i, j, PCOL_G1 // 8)),
                  pl.BlockSpec((None, tm, DN_WIDTH), lambda i, j: (i, j, 0)),
                  pl.BlockSpec((None, N_GROUPS, tm, GROUP_DIM), lambda i, j: (i, 0, j, 0)),
                  pl.BlockSpec((None, 1, d), lambda i, j: (i, 0, 2)),
                  pl.BlockSpec((1, d), lambda i, j: (0, 0)),
                  _resident(wf.shape), _resident(wd.shape), _resident(wo.shape)],
        out_specs=pl.BlockSpec((None, tm, d), lambda i, j: (i, j, 0)),
        out_shape=jax.ShapeDtypeStruct((b, t, d), F32),
        compiler_params=_cparams("arbitrary", "arbitrary"),
        name="merge",
    )(x, p, p, o, four, mod3, nw, wf, wd, wo)


def _ffn_kernel(x_ref, xp_ref, xn_ref, sh_ref, sc_ref, gt_ref, npre_ref, npost_ref,
                wa_ref, wu_ref, cw_ref, wd_ref, out_ref, hb_ref, a_ref, acc_ref):
    tm = x_ref.shape[0]
    gw = GRID_W
    i = pl.program_id(1)
    last = pl.num_programs(1) - 1
    nf = wa_ref.shape[0]

    def nm(xt):
        return _norm_mod(xt, npre_ref[...], sh_ref[...], sc_ref[...]).astype(BF16)

    hb_ref[0:gw, :] = nm(xp_ref[...])
    for r0 in range(0, tm, 128):
        hb_ref[gw + r0:gw + r0 + 128, :] = nm(x_ref[r0:r0 + 128, :])
    hb_ref[gw + tm:, :] = nm(xn_ref[...])
    acc_ref[...] = jnp.zeros_like(acc_ref)
    top = jnp.where(i > 0, 1.0, 0.0)
    bot = jnp.where(i < last, 1.0, 0.0)
    colid = lax.broadcasted_iota(jnp.int32, (tm, 1), 0) % gw
    has_l = colid != 0
    has_r = colid != gw - 1

    def body(fj, carry):
        a = _dot(hb_ref[...], wa_ref[fj])
        a_ref[0:gw, :] = a[0:gw] * top
        a_ref[gw:gw + tm, :] = a[gw:gw + tm]
        a_ref[gw + tm:, :] = a[gw + tm:] * bot
        cw = cw_ref[fj]
        s0 = a_ref[0:tm, :]
        s1 = a_ref[gw:gw + tm, :]
        s2 = a_ref[2 * gw:2 * gw + tm, :]
        left = cw[0:1] * s0 + cw[3:4] * s1 + cw[6:7] * s2
        mid = cw[1:2] * s0 + cw[4:5] * s1 + cw[7:8] * s2
        right = cw[2:3] * s0 + cw[5:6] * s1 + cw[8:9] * s2
        conv = (mid + jnp.where(has_l, pltpu.roll(left, 1, 0), 0.0)
                + jnp.where(has_r, pltpu.roll(right, tm - 1, 0), 0.0))
        u = _dot(hb_ref[gw:gw + tm, :], wu_ref[fj])
        act = (_silu(conv) * u).astype(BF16)
        acc_ref[...] += _dot(act, wd_ref[fj])
        return carry

    lax.fori_loop(0, nf, body, 0)
    for r0 in range(0, tm, 128):
        sl = slice(r0, r0 + 128)
        y = acc_ref[sl, :]
        yn = y * lax.rsqrt(jnp.mean(y * y, axis=-1, keepdims=True) + NORM_EPS) * npost_ref[...]
        out_ref[sl, :] = x_ref[sl, :] + gt_ref[...] * yn


def _ffn(x, mod3, npre, npost, wa, wu, cw, wd, tm):
    b, t, d = x.shape
    gw = GRID_W
    rpt = tm // gw
    nrow = t // gw
    nf, _, tf = wa.shape
    return pl.pallas_call(
        _ffn_kernel,
        grid=(b, t // tm),
        in_specs=[pl.BlockSpec((None, tm, d), lambda i, j: (i, j, 0)),
                  pl.BlockSpec((None, gw, d), lambda i, j: (i, jnp.maximum(j * rpt - 1, 0), 0)),
                  pl.BlockSpec((None, gw, d), lambda i, j: (i, jnp.minimum(j * rpt + rpt, nrow - 1), 0)),
                  pl.BlockSpec((None, 1, d), lambda i, j: (i, 0, 3)),
                  pl.BlockSpec((None, 1, d), lambda i, j: (i, 0, 4)),
                  pl.BlockSpec((None, 1, d), lambda i, j: (i, 0, 5)),
                  pl.BlockSpec((1, d), lambda i, j: (0, 0)),
                  pl.BlockSpec((1, d), lambda i, j: (0, 0)),
                  _resident(wa.shape), _resident(wu.shape), _resident(cw.shape), _resident(wd.shape)],
        out_specs=pl.BlockSpec((None, tm, d), lambda i, j: (i, j, 0)),
        out_shape=jax.ShapeDtypeStruct((b, t, d), F32),
        scratch_shapes=[pltpu.VMEM((tm + 2 * gw, d), BF16),
                        pltpu.VMEM((tm + 2 * gw, tf), F32),
                        pltpu.VMEM((tm, d), F32)],
        compiler_params=_cparams("arbitrary", "arbitrary"),
        name="ffn",
    )(x, x, x, mod3, mod3, mod3, npre, npost, wa, wu, cw, wd)


def kernel(x, c, ctx, c_ctx, w_ada, b_ada, norm_pre_mix, norm_post_mix, norm_pre_ffn, norm_post_ffn,
           w_in, conv_qkv, a_log, dt_bias, dn_norm, w_fourier, w_dn, w_out, w_up, conv_ffn, w_down):
    assert w_ada.shape[0] == 1, "single-layer stack"
    bsz, seq, d = x.shape
    d_ff = w_down.shape[1]
    assert d == DN_WIDTH and seq % (FFT_N1 * 16) == 0 and seq % 512 == 0 and ctx.shape[1] % CHUNK == 0

    pad = (-(bsz + 1)) % 8
    c_rows = jnp.concatenate([c, c_ctx[None, :], jnp.zeros((pad, d), F32)], axis=0)
    mod = _ada(c_rows, w_ada[0], b_ada)
    mod3 = mod.reshape(mod.shape[0], 1, 6 * d)

    wi = w_in[0]
    o_q = F_WIDTH
    o_z = o_q + 3 * DN_WIDTH
    o_b = o_z + DN_WIDTH
    o_g = o_b + 2 * N_DIR * HEADS
    w_main = jnp.concatenate([wi[:, o_g:], wi[:, o_z:o_b], wi[:, o_q:o_z], wi[:, :o_q]], axis=1).astype(BF16)
    w_ba = jnp.pad(wi[:, o_b:o_g], ((0, 0), (0, 128 - 2 * N_DIR * HEADS))).astype(BF16)
    npm = norm_pre_mix
    p_c, ba_c = _inproj(ctx, mod3, lambda i: bsz, npm, w_main, w_ba, tm=ctx.shape[1])
    p_x, ba_x = _inproj(x, mod3, lambda i: i, npm, w_main, w_ba, tm=512)

    lanes = jnp.zeros((1, 128), F32)
    alog_row = lanes.at[0, 16:32].set(a_log[0].reshape(-1))
    dtb_row = lanes.at[0, 16:32].set(dt_bias[0].reshape(-1))
    zero = jnp.zeros((N_DIR, bsz, HEADS, HEAD_DIM, HEAD_DIM), F32)
    (s_ctx,) = _deltanet(p_c, ba_c, conv_qkv[0], alog_row, dtb_row, dn_norm, zero, emit_o=False)
    o_x, _ = _deltanet(p_x, ba_x, conv_qkv[0], alog_row, dtb_row, dn_norm, s_ctx, emit_o=True)

    four = _fourier(p_x)
    x1 = _merge(x, p_x, o_x, four, mod3, norm_post_mix, w_fourier[0].astype(BF16),
                w_dn[0].astype(BF16), w_out[0].astype(BF16), tm=512)

    tf = 256
    nf = d_ff // tf
    wup = w_up[0].astype(BF16)
    wa = wup[:, :d_ff].reshape(d, nf, tf).transpose(1, 0, 2)
    wu = wup[:, d_ff:].reshape(d, nf, tf).transpose(1, 0, 2)
    cw = conv_ffn[0].reshape(9, nf, tf).transpose(1, 0, 2)
    wd = w_down[0].astype(BF16).reshape(nf, tf, d)
    return _ffn(x1, mod3, norm_pre_ffn, norm_post_ffn, wa, wu, cw, wd, tm=512)
```

```python
import functools
import math

import numpy as np
import jax
import jax.numpy as jnp
from jax import lax
from jax.experimental import pallas as pl
from jax.experimental.pallas import tpu as pltpu

F32 = jnp.float32
BF16 = jnp.bfloat16

GRID_W = 64
N_GROUPS = 4
GROUP_DIM = 128
F_WIDTH = N_GROUPS * GROUP_DIM
HEADS = 8
HEAD_DIM = 128
DN_WIDTH = HEADS * HEAD_DIM
N_DIR = 2
NORM_EPS = 1e-6
L2_EPS = 1e-6
CHUNK = 128
LOG2_CHUNK = 7
FFT_N1 = 8
VMEM_LIMIT = 56 * 1024 * 1024


def _cparams(*sem):
    return pltpu.CompilerParams(dimension_semantics=sem, vmem_limit_bytes=VMEM_LIMIT)


def _sigmoid(x):
    return 1.0 / (1.0 + jnp.exp(-x))


def _silu(x):
    return x * _sigmoid(x)


def _softplus(x):
    return jnp.maximum(x, 0.0) + jnp.log1p(jnp.exp(-jnp.abs(x)))


def _dot(a, b):
    return jnp.dot(a, b, preferred_element_type=F32)


def _dot_nt(a, b):
    return lax.dot_general(a, b, (((1,), (1,)), ((), ())), preferred_element_type=F32)


def _split3(x):
    x1 = x.astype(BF16)
    r1 = x - x1.astype(F32)
    x2 = r1.astype(BF16)
    x3 = (r1 - x2.astype(F32)).astype(BF16)
    return x1, x2, x3


def _resident(shape):
    nd = len(shape)
    return pl.BlockSpec(shape, lambda *_: (0,) * nd, pipeline_mode=pl.Buffered(1))


def _ada_kernel(c_ref, w_ref, b_ref, o_ref):
    s = _silu(c_ref[...])
    o_ref[...] = jnp.dot(s, w_ref[...], preferred_element_type=F32,
                         precision=lax.Precision.HIGHEST) + b_ref[...]


def _ada(c_rows, w_ada, b_ada):
    rows, d = c_rows.shape
    n = w_ada.shape[1]
    tn = n // 4
    return pl.pallas_call(
        _ada_kernel,
        grid=(n // tn,),
        in_specs=[pl.BlockSpec((rows, d), lambda j: (0, 0)),
                  pl.BlockSpec((d, tn), lambda j: (0, j)),
                  pl.BlockSpec((1, tn), lambda j: (0, j))],
        out_specs=pl.BlockSpec((rows, tn), lambda j: (0, j)),
        out_shape=jax.ShapeDtypeStruct((rows, n), F32),
        compiler_params=_cparams("arbitrary"),
        name="ada",
    )(c_rows, w_ada, b_ada)


def _norm_mod(x, nw, sh, sc):
    ms = jnp.mean(x * x, axis=-1, keepdims=True)
    y = x * lax.rsqrt(ms + NORM_EPS) * nw
    return y * (1.0 + sc) + sh


def _inproj_kernel(x_ref, sh_ref, sc_ref, nw_ref, w_ref, wba_ref, p_ref, ba_ref, hb_ref, *, tn, rows):
    tm = x_ref.shape[0]
    for r0 in range(0, tm, rows):
        h = _norm_mod(x_ref[r0:r0 + rows, :], nw_ref[...], sh_ref[...], sc_ref[...])
        hb_ref[r0:r0 + rows, :] = h.astype(BF16)
    hb = hb_ref[...]
    for j in range(w_ref.shape[1] // tn):
        p_ref[:, j * tn:(j + 1) * tn] = _dot(hb, w_ref[:, j * tn:(j + 1) * tn]).astype(BF16)
    ba_ref[...] = _dot(hb, wba_ref[...])


def _inproj(x, mod3, mod_row, nw, w_main, w_ba, tm):
    b, t, d = x.shape
    nc = w_main.shape[1]
    kern = functools.partial(_inproj_kernel, tn=512, rows=128)
    return pl.pallas_call(
        kern,
        grid=(b, t // tm),
        in_specs=[pl.BlockSpec((None, tm, d), lambda i, j: (i, j, 0)),
                  pl.BlockSpec((None, 1, d), lambda i, j: (mod_row(i), 0, 0)),
                  pl.BlockSpec((None, 1, d), lambda i, j: (mod_row(i), 0, 1)),
                  pl.BlockSpec((1, d), lambda i, j: (0, 0)),
                  _resident((d, nc)),
                  _resident((d, 128))],
        out_specs=[pl.BlockSpec((None, tm, nc), lambda i, j: (i, j, 0)),
                   pl.BlockSpec((None, tm, 128), lambda i, j: (i, j, 0))],
        out_shape=[jax.ShapeDtypeStruct((b, t, nc), BF16),
                   jax.ShapeDtypeStruct((b, t, 128), F32)],
        scratch_shapes=[pltpu.VMEM((tm, d), BF16)],
        compiler_params=_cparams("arbitrary", "arbitrary"),
        name="inproj",
    )(x, mod3, mod3, nw, w_main, w_ba)


PCOL_G0 = 0
PCOL_G1 = 8
PCOL_Z = 16
PCOL_Q = 24
PCOL_K = 32
PCOL_V = 40
PCOL_F = 48
P_WIDTH = 52 * 128


def _conv_block(x_ref, w_ref, r, nblk):
    t = x_ref.shape[0]
    r0 = pl.multiple_of(r * CHUNK, CHUNK)
    main = x_ref[pl.ds(r0, CHUNK), :].astype(F32)
    p0 = pl.multiple_of(jnp.maximum(r0 - 16, 0), 16)
    n0 = pl.multiple_of(jnp.minimum(r0 + CHUNK, t - 16), 16)
    prev = x_ref[pl.ds(p0, 16), :].astype(F32)[15:16, :]
    nxt = x_ref[pl.ds(n0, 16), :].astype(F32)[0:1, :]
    prev = jnp.where(r > 0, prev, 0.0)
    nxt = jnp.where(r < nblk - 1, nxt, 0.0)
    row = lax.broadcasted_iota(jnp.int32, (CHUNK, HEAD_DIM), 0)
    dn = jnp.where(row == 0, prev, pltpu.roll(main, 1, 0))
    up = jnp.where(row == CHUNK - 1, nxt, pltpu.roll(main, CHUNK - 1, 0))
    w = w_ref[...]
    y = w[0:1, :] * dn + w[1:2, :] * main + w[2:3, :] * up
    return _silu(y)


def _l2norm(y):
    return y * lax.rsqrt(jnp.sum(y * y, axis=-1, keepdims=True) + L2_EPS)


def _deltanet_kernel(q_ref, k_ref, v_ref, z_ref, ba_ref, wq_ref, wk_ref, wv_ref, alog_ref, dtb_ref,
                     dnw_ref, s0_ref, *rest, emit_o, unroll):
    if emit_o:
        o_ref, sout_ref = rest[0], rest[1]
        mq_ref, n_ref, oacc_ref, s_ref = rest[2:]
    else:
        sout_ref = rest[0]
        mq_ref, n_ref, s_ref = rest[1:]
    t = q_ref.shape[0]
    nblk = t // CHUNK
    h = pl.program_id(1)

    row_i = lax.broadcasted_iota(jnp.int32, (CHUNK, CHUNK), 0)
    col_i = lax.broadcasted_iota(jnp.int32, (CHUNK, CHUNK), 1)
    lane = col_i
    tri_lo = (row_i >= col_i).astype(BF16)
    a_neg = -jnp.exp(alog_ref[...])
    dtb = dtb_ref[...]
    xor = row_i ^ col_i
    lvl = jnp.zeros((CHUNK, CHUNK), jnp.int32)
    for l in range(1, LOG2_CHUNK):
        lvl = jnp.where(xor >= (1 << l), l, lvl)
    eye = (row_i == col_i).astype(F32)
    sub8 = lax.broadcasted_iota(jnp.int32, (HEADS, CHUNK), 0)

    def block_terms(blk):
        r0 = pl.multiple_of(blk * CHUNK, CHUNK)
        q = _l2norm(_conv_block(q_ref, wq_ref, blk, nblk)) * (HEAD_DIM ** -0.5)
        k = _l2norm(_conv_block(k_ref, wk_ref, blk, nblk))
        v = _conv_block(v_ref, wv_ref, blk, nblk)
        raw = ba_ref[pl.ds(r0, CHUNK), :]
        gate = jnp.where(lane < 16, _sigmoid(raw), a_neg * _softplus(raw + dtb))
        return dict(blk=blk, r0=r0, q=q, k=k, v=v, kt=k.T, kb=k.astype(BF16), gate=gate)

    def chain_setup(bt, d):
        gc, gct, kk = bt["gc"], bt["gct"], bt["kk"]
        beta = jnp.sum(jnp.where(lane == 8 * d + h, gc, 0.0), axis=1, keepdims=True)
        gcol = jnp.sum(jnp.where(lane == 16 + 8 * d + h, gc, 0.0), axis=1, keepdims=True)
        grow = jnp.sum(jnp.where(sub8 == h, gct[16 + 8 * d:24 + 8 * d, :], 0.0),
                       axis=0, keepdims=True)
        if d == 0:
            incl, strict = row_i >= col_i, row_i > col_i
            glast = gcol[CHUNK - 1:CHUNK, :]
        else:
            incl, strict = row_i <= col_i, row_i < col_i
            glast = gcol[0:1, :]
        decay = jnp.exp(jnp.where(incl, gcol - grow, -jnp.inf))
        lmat = jnp.where(strict, beta * kk * decay, 0.0)
        egc = jnp.exp(gcol)
        rhs = jnp.concatenate([bt["k"] * (beta * egc), bt["v"] * beta], axis=1).astype(BF16)
        ktail = (bt["kt"] * jnp.exp(glast - grow)).astype(BF16)
        return dict(bt=bt, d=d, decay=decay, lmat=lmat, egc=egc, rhs=rhs, ktail=ktail, glast=glast,
                    tinv=eye - jnp.where(lvl == 0, lmat, 0.0))

    def pass_a(i, carry):
        blocks = [block_terms(i * unroll + j) for j in range(unroll)]
        for bt in blocks:
            bt["g3"] = _split3(bt["gate"])
        for bt in blocks:
            bt["pre"] = sum(_dot(tri_lo, g) for g in bt["g3"])
            bt["suf"] = bt["pre"][CHUNK - 1:CHUNK, :] - bt["pre"] + bt["gate"]
        for bt in blocks:
            bt["kk"] = _dot_nt(bt["kb"], bt["kb"])
            if emit_o:
                bt["qkt"] = _dot_nt(bt["q"].astype(BF16), bt["kb"])
        for bt in blocks:
            bt["gc"] = jnp.where(lane < 16, bt["gate"], jnp.where(lane < 24, bt["pre"], bt["suf"]))
            bt["gct"] = bt["gc"].T
        chains = [chain_setup(bt, d) for bt in blocks for d in range(N_DIR)]
        for l in range(1, LOG2_CHUNK):
            for c in chains:
                c["tb"] = c["tinv"].astype(BF16)
                c["x"] = _dot(c["tb"], jnp.where(lvl == l, c["lmat"], 0.0).astype(BF16)).astype(BF16)
            for c in chains:
                c["tinv"] = c["tinv"] - _dot(c["x"], c["tb"])
        for c in chains:
            c["wu"] = _dot(c["tinv"].astype(BF16), c["rhs"]).astype(BF16)
        for c in chains:
            c["ku"] = _dot(c["ktail"], c["wu"])
            if emit_o:
                c["qu"] = _dot((c["bt"]["qkt"] * c["decay"]).astype(BF16), c["wu"])
        for c in chains:
            d, bt = c["d"], c["bt"]
            m = eye * jnp.exp(c["glast"]) - c["ku"][:, :HEAD_DIM]
            n_ref[d, bt["blk"]] = c["ku"][:, HEAD_DIM:]
            if emit_o:
                qe = bt["q"] * c["egc"] - c["qu"][:, :HEAD_DIM]
                mq_ref[d, bt["blk"]] = jnp.concatenate([m, qe], axis=0).astype(BF16)
                oacc_ref[d, pl.ds(bt["r0"], CHUNK), :] = c["qu"][:, HEAD_DIM:]
            else:
                mq_ref[d, bt["blk"]] = m.astype(BF16)
        return carry

    lax.fori_loop(0, nblk // unroll, pass_a, 0)

    s_ref[...] = s0_ref[...]

    def pass_b(i, carry):
        for d in range(N_DIR):
            blk = i if d == 0 else nblk - 1 - i
            r = _dot(mq_ref[d, blk], s_ref[d].astype(BF16))
            if emit_o:
                r0 = pl.multiple_of(blk * CHUNK, CHUNK)
                oacc_ref[d, pl.ds(r0, CHUNK), :] += r[CHUNK:]
            s_ref[d] = r[:CHUNK] + n_ref[d, blk]
        return carry

    lax.fori_loop(0, nblk, pass_b, 0)
    sout_ref[...] = s_ref[...]

    if emit_o:
        def fin(r, carry):
            r0 = pl.multiple_of(r * CHUNK, CHUNK)
            o = oacc_ref[0, pl.ds(r0, CHUNK), :] + oacc_ref[1, pl.ds(r0, CHUNK), :]
            on = o * lax.rsqrt(jnp.mean(o * o, axis=-1, keepdims=True) + NORM_EPS) * dnw_ref[...]
            z = z_ref[pl.ds(r0, CHUNK), :].astype(F32)
            o_ref[pl.ds(r0, CHUNK), :] = (on * _silu(z)).astype(BF16)
            return carry

        lax.fori_loop(0, nblk, fin, 0, unroll=unroll)


def _deltanet(p, ba, convw, alog_row, dtb_row, dnw, s0, emit_o):
    b, t, _ = p.shape
    nblk = t // CHUNK
    unroll = 8 if nblk % 8 == 0 else (2 if nblk % 2 == 0 else 1)
    tok = lambda c0: pl.BlockSpec((None, t, 128), lambda i, j: (i, 0, c0 + j))
    cw = lambda c0: pl.BlockSpec((3, 128), lambda i, j: (0, c0 + j))
    par = pl.BlockSpec((1, 128), lambda i, j: (0, 0))
    st = pl.BlockSpec((N_DIR, None, None, HEAD_DIM, HEAD_DIM), lambda i, j: (0, i, j, 0, 0))
    out_specs = [st]
    out_shape = [jax.ShapeDtypeStruct((N_DIR, b, HEADS, HEAD_DIM, HEAD_DIM), F32)]
    scratch = [pltpu.VMEM((N_DIR, nblk, (2 if emit_o else 1) * CHUNK, HEAD_DIM), BF16),
               pltpu.VMEM((N_DIR, nblk, HEAD_DIM, HEAD_DIM), F32)]
    if emit_o:
        out_specs = [pl.BlockSpec((None, t, 128), lambda i, j: (i, 0, j))] + out_specs
        out_shape = [jax.ShapeDtypeStruct((b, t, DN_WIDTH), BF16)] + out_shape
        scratch.append(pltpu.VMEM((N_DIR, t, HEAD_DIM), F32))
    scratch.append(pltpu.VMEM((N_DIR, HEAD_DIM, HEAD_DIM), F32))
    return pl.pallas_call(
        functools.partial(_deltanet_kernel, emit_o=emit_o, unroll=unroll),
        grid=(b, HEADS),
        in_specs=[tok(PCOL_Q), tok(PCOL_K), tok(PCOL_V), tok(PCOL_Z),
                  pl.BlockSpec((None, t, 128), lambda i, j: (i, 0, 0)),
                  cw(0), cw(HEADS), cw(2 * HEADS), par, par, par, st],
        out_specs=out_specs,
        out_shape=out_shape,
        scratch_shapes=scratch,
        compiler_params=_cparams("arbitrary", "arbitrary"),
        name="deltanet_x" if emit_o else "deltanet_ctx",
    )(p, p, p, p, ba, convw, convw, convw, alog_row, dtb_row, dnw, s0)


def _fourier_kernel(x_ref, wc_ref, twc_ref, tws_ref, cn_ref, sn_ref, o_ref, y_ref, a_ref, *, rows):
    n1 = FFT_N1
    t = x_ref.shape[0]
    n2 = t // n1
    fw = F_WIDTH
    wc = wc_ref[...].astype(BF16)
    for t1 in range(n1):
        y_ref[t1] = _dot(x_ref[t1 * n2:(t1 + 1) * n2, :], wc).astype(BF16)

    ang = [2.0 * math.pi * m / n1 for m in range(n1)]
    snap = lambda val: float(round(val)) if abs(val - round(val)) < 1e-9 else val
    cs = [(snap(math.cos(a)), snap(math.sin(a))) for a in ang]

    def axpy(acc, coef, val):
        if coef == 0.0:
            return acc
        if acc is None:
            return val if coef == 1.0 else (-val if coef == -1.0 else coef * val)
        if coef == 1.0:
            return acc + val
        if coef == -1.0:
            return acc - val
        return acc + coef * val

    hw = 256

    def stage1(rb, carry):
        r0 = pl.multiple_of(rb * rows, rows)
        for l0 in range(0, fw, hw):
            yr = [y_ref[t1, pl.ds(r0, rows), l0:l0 + hw].astype(F32) for t1 in range(n1)]
            yi = [y_ref[t1, pl.ds(r0, rows), fw + l0:fw + l0 + hw].astype(F32) for t1 in range(n1)]
            for k1 in range(n1):
                ar = None
                ai = None
                for t1 in range(n1):
                    c, s = cs[(t1 * k1) % n1]
                    ar = axpy(axpy(ar, c, yr[t1]), s, yi[t1])
                    ai = axpy(axpy(ai, c, yi[t1]), -s, yr[t1])
                twc = jnp.concatenate([twc_ref[k1, pl.ds(r0, rows), :]] * (hw // 128), axis=1)
                tws = jnp.concatenate([tws_ref[k1, pl.ds(r0, rows), :]] * (hw // 128), axis=1)
                a_ref[k1, 0, pl.ds(r0, rows), l0:l0 + hw] = (ar * twc + ai * tws).astype(BF16)
                a_ref[k1, 1, pl.ds(r0, rows), l0:l0 + hw] = (ai * twc - ar * tws).astype(BF16)
        return carry

    lax.fori_loop(0, n2 // rows, stage1, 0)

    cn = cn_ref[...].astype(BF16)
    sn = sn_ref[...].astype(BF16)
    for k1 in range(n1):
        zr = _dot(cn, a_ref[k1, 0]) + _dot(sn, a_ref[k1, 1])
        for g in range(N_GROUPS):
            o_ref[g, pl.ds(k1, n2, stride=n1), :] = zr[:, g * GROUP_DIM:(g + 1) * GROUP_DIM]


def _fourier_consts(t):
    n1 = FFT_N1
    n2 = t // n1
    scale = 1.0 / math.sqrt(t * GROUP_DIM)
    j = np.arange(GROUP_DIM)
    ang = 2.0 * np.pi * np.outer(j, j) / GROUP_DIM
    wc = np.zeros((F_WIDTH, 2 * F_WIDTH), np.float32)
    for g in range(N_GROUPS):
        sl = slice(g * GROUP_DIM, (g + 1) * GROUP_DIM)
        wc[sl, sl] = np.cos(ang) * scale
        wc[sl, F_WIDTH + g * GROUP_DIM:F_WIDTH + (g + 1) * GROUP_DIM] = -np.sin(ang) * scale
    t2 = np.arange(n2)
    k1 = np.arange(n1)
    tw = 2.0 * np.pi * np.outer(k1, t2) / t
    a2 = 2.0 * np.pi * np.outer(t2, t2) / n2
    return (wc, np.cos(tw).astype(np.float32), np.sin(tw).astype(np.float32),
            np.cos(a2).astype(np.float32), np.sin(a2).astype(np.float32))


def _fourier(p):
    b, t, _ = p.shape
    n1 = FFT_N1
    n2 = t // n1
    wc, twc, tws, cn, sn = _fourier_consts(t)
    wc, cn, sn = jnp.asarray(wc), jnp.asarray(cn), jnp.asarray(sn)
    twc = jnp.broadcast_to(jnp.asarray(twc)[:, :, None], (n1, n2, 128))
    tws = jnp.broadcast_to(jnp.asarray(tws)[:, :, None], (n1, n2, 128))
    return pl.pallas_call(
        functools.partial(_fourier_kernel, rows=16),
        grid=(b,),
        in_specs=[pl.BlockSpec((None, t, F_WIDTH), lambda i: (i, 0, PCOL_F // 4)),
                  _resident((F_WIDTH, 2 * F_WIDTH)),
                  _resident((n1, n2, 128)), _resident((n1, n2, 128)),
                  _resident((n2, n2)), _resident((n2, n2))],
        out_specs=pl.BlockSpec((None, N_GROUPS, t, GROUP_DIM), lambda i: (i, 0, 0, 0)),
        out_shape=jax.ShapeDtypeStruct((b, N_GROUPS, t, GROUP_DIM), F32),
        scratch_shapes=[pltpu.VMEM((n1, n2, 2 * F_WIDTH), BF16),
                        pltpu.VMEM((n1, 2, n2, F_WIDTH), BF16)],
        compiler_params=_cparams("arbitrary"),
        name="fourier",
    )(p, wc, twc, tws, cn, sn)


def _merge_kernel(x_ref, g0_ref, g1_ref, o_ref, f_ref, gt_ref, nw_ref, wf_ref, wd_ref, wo_ref, out_ref,
                  *, rows):
    tm = x_ref.shape[0]
    for r0 in range(0, tm, rows):
        sl = slice(r0, r0 + rows)
        fo = jnp.concatenate([f_ref[g, sl, :] for g in range(N_GROUPS)], axis=1).astype(BF16)
        yf = _dot(fo, wf_ref[...])
        yd = _dot(o_ref[sl, :], wd_ref[...])
        m = _sigmoid(g0_ref[sl, :].astype(F32)) * yf + _sigmoid(g1_ref[sl, :].astype(F32)) * yd
        y = _dot(m.astype(BF16), wo_ref[...])
        yn = y * lax.rsqrt(jnp.mean(y * y, axis=-1, keepdims=True) + NORM_EPS) * nw_ref[...]
        out_ref[sl, :] = x_ref[sl, :] + gt_ref[...] * yn


def _merge(x, p, o, four, mod3, nw, wf, wd, wo, tm):
    b, t, d = x.shape
    return pl.pallas_call(
        functools.partial(_merge_kernel, rows=256),
        grid=(b, t // tm),
        in_specs=[pl.BlockSpec((None, tm, d), lambda i, j: (i, j, 0)),
                  pl.BlockSpec((None, tm, d), lambda i, j: (i, j, PCOL_G0 // 8)),
                  pl.BlockSpec((None, tm, d), lambda i, j: (i, j, PCOL_G1 // 8)),
                  pl.BlockSpec((None, tm, DN_WIDTH), lambda i, j: (i, j, 0)),
                  pl.BlockSpec((None, N_GROUPS, tm, GROUP_DIM), lambda i, j: (i, 0, j, 0)),
                  pl.BlockSpec((None, 1, d), lambda i, j: (i, 0, 2)),
                  pl.BlockSpec((1, d), lambda i, j: (0, 0)),
                  _resident(wf.shape), _resident(wd.shape), _resident(wo.shape)],
        out_specs=pl.BlockSpec((None, tm, d), lambda i, j: (i, j, 0)),
        out_shape=jax.ShapeDtypeStruct((b, t, d), F32),
        compiler_params=_cparams("arbitrary", "arbitrary"),
        name="merge",
    )(x, p, p, o, four, mod3, nw, wf, wd, wo)


def _ffn_kernel(x_ref, xp_ref, xn_ref, sh_ref, sc_ref, gt_ref, npre_ref, npost_ref,
                wa_ref, wu_ref, cw_ref, wd_ref, out_ref, hb_ref, a_ref, a2_ref, acc_ref):
    tm = x_ref.shape[0]
    gw = GRID_W
    i = pl.program_id(1)
    last = pl.num_programs(1) - 1
    nf = wa_ref.shape[0]

    def nm(xt):
        return _norm_mod(xt, npre_ref[...], sh_ref[...], sc_ref[...]).astype(BF16)

    hb_ref[0:gw, :] = nm(xp_ref[...])
    for r0 in range(0, tm, 128):
        hb_ref[gw + r0:gw + r0 + 128, :] = nm(x_ref[r0:r0 + 128, :])
    hb_ref[gw + tm:, :] = nm(xn_ref[...])
    acc_ref[...] = jnp.zeros_like(acc_ref)
    top = jnp.where(i > 0, 1.0, 0.0)
    bot = jnp.where(i < last, 1.0, 0.0)
    colid = lax.broadcasted_iota(jnp.int32, (tm, 1), 0) % gw
    has_l = colid != 0
    has_r = colid != gw - 1

    def up_a(dst_ref, fj):
        a = _dot(hb_ref[...], wa_ref[fj])
        dst_ref[0:gw, :] = a[0:gw] * top
        dst_ref[gw:gw + tm, :] = a[gw:gw + tm]
        dst_ref[gw + tm:, :] = a[gw + tm:] * bot

    def consume(src_ref, fj):
        cw = cw_ref[fj]
        s0 = src_ref[0:tm, :]
        s1 = src_ref[gw:gw + tm, :]
        s2 = src_ref[2 * gw:2 * gw + tm, :]
        left = cw[0:1] * s0 + cw[3:4] * s1 + cw[6:7] * s2
        mid = cw[1:2] * s0 + cw[4:5] * s1 + cw[7:8] * s2
        right = cw[2:3] * s0 + cw[5:6] * s1 + cw[8:9] * s2
        conv = (mid + jnp.where(has_l, pltpu.roll(left, 1, 0), 0.0)
                + jnp.where(has_r, pltpu.roll(right, tm - 1, 0), 0.0))
        u = _dot(hb_ref[gw:gw + tm, :], wu_ref[fj])
        act = (_silu(conv) * u).astype(BF16)
        acc_ref[...] += _dot(act, wd_ref[fj])

    up_a(a_ref, 0)

    def body(i, carry):
        f0 = 2 * i
        up_a(a2_ref, f0 + 1)
        consume(a_ref, f0)
        up_a(a_ref, jnp.minimum(f0 + 2, nf - 1))
        consume(a2_ref, f0 + 1)
        return carry

    lax.fori_loop(0, nf // 2, body, 0)
    if nf % 2 == 1:
        consume(a_ref, nf - 1)
    for r0 in range(0, tm, 128):
        sl = slice(r0, r0 + 128)
        y = acc_ref[sl, :]
        yn = y * lax.rsqrt(jnp.mean(y * y, axis=-1, keepdims=True) + NORM_EPS) * npost_ref[...]
        out_ref[sl, :] = x_ref[sl, :] + gt_ref[...] * yn


def _ffn(x, mod3, npre, npost, wa, wu, cw, wd, tm):
    b, t, d = x.shape
    gw = GRID_W
    rpt = tm // gw
    nrow = t // gw
    nf, _, tf = wa.shape
    return pl.pallas_call(
        _ffn_kernel,
        grid=(b, t // tm),
        in_specs=[pl.BlockSpec((None, tm, d), lambda i, j: (i, j, 0)),
                  pl.BlockSpec((None, gw, d), lambda i, j: (i, jnp.maximum(j * rpt - 1, 0), 0)),
                  pl.BlockSpec((None, gw, d), lambda i, j: (i, jnp.minimum(j * rpt + rpt, nrow - 1), 0)),
                  pl.BlockSpec((None, 1, d), lambda i, j: (i, 0, 3)),
                  pl.BlockSpec((None, 1, d), lambda i, j: (i, 0, 4)),
                  pl.BlockSpec((None, 1, d), lambda i, j: (i, 0, 5)),
                  pl.BlockSpec((1, d), lambda i, j: (0, 0)),
                  pl.BlockSpec((1, d), lambda i, j: (0, 0)),
                  _resident(wa.shape), _resident(wu.shape), _resident(cw.shape), _resident(wd.shape)],
        out_specs=pl.BlockSpec((None, tm, d), lambda i, j: (i, j, 0)),
        out_shape=jax.ShapeDtypeStruct((b, t, d), F32),
        scratch_shapes=[pltpu.VMEM((tm + 2 * gw, d), BF16),
                        pltpu.VMEM((tm + 2 * gw, tf), F32),
                        pltpu.VMEM((tm + 2 * gw, tf), F32),
                        pltpu.VMEM((tm, d), F32)],
        compiler_params=_cparams("arbitrary", "arbitrary"),
        name="ffn",
    )(x, x, x, mod3, mod3, mod3, npre, npost, wa, wu, cw, wd)


def kernel(x, c, ctx, c_ctx, w_ada, b_ada, norm_pre_mix, norm_post_mix, norm_pre_ffn, norm_post_ffn,
           w_in, conv_qkv, a_log, dt_bias, dn_norm, w_fourier, w_dn, w_out, w_up, conv_ffn, w_down):
    assert w_ada.shape[0] == 1, "single-layer stack"
    bsz, seq, d = x.shape
    d_ff = w_down.shape[1]
    assert d == DN_WIDTH and seq % (FFT_N1 * 16) == 0 and seq % 512 == 0 and ctx.shape[1] % CHUNK == 0

    pad = (-(bsz + 1)) % 8
    c_rows = jnp.concatenate([c, c_ctx[None, :], jnp.zeros((pad, d), F32)], axis=0)
    mod = _ada(c_rows, w_ada[0], b_ada)
    mod3 = mod.reshape(mod.shape[0], 1, 6 * d)

    wi = w_in[0]
    o_q = F_WIDTH
    o_z = o_q + 3 * DN_WIDTH
    o_b = o_z + DN_WIDTH
    o_g = o_b + 2 * N_DIR * HEADS
    w_main = jnp.concatenate([wi[:, o_g:], wi[:, o_z:o_b], wi[:, o_q:o_z], wi[:, :o_q]], axis=1).astype(BF16)
    w_ba = jnp.pad(wi[:, o_b:o_g], ((0, 0), (0, 128 - 2 * N_DIR * HEADS))).astype(BF16)
    npm = norm_pre_mix
    p_c, ba_c = _inproj(ctx, mod3, lambda i: bsz, npm, w_main, w_ba, tm=ctx.shape[1])
    p_x, ba_x = _inproj(x, mod3, lambda i: i, npm, w_main, w_ba, tm=512)

    lanes = jnp.zeros((1, 128), F32)
    alog_row = lanes.at[0, 16:32].set(a_log[0].reshape(-1))
    dtb_row = lanes.at[0, 16:32].set(dt_bias[0].reshape(-1))
    zero = jnp.zeros((N_DIR, bsz, HEADS, HEAD_DIM, HEAD_DIM), F32)
    (s_ctx,) = _deltanet(p_c, ba_c, conv_qkv[0], alog_row, dtb_row, dn_norm, zero, emit_o=False)
    o_x, _ = _deltanet(p_x, ba_x, conv_qkv[0], alog_row, dtb_row, dn_norm, s_ctx, emit_o=True)

    four = _fourier(p_x)
    x1 = _merge(x, p_x, o_x, four, mod3, norm_post_mix, w_fourier[0].astype(BF16),
                w_dn[0].astype(BF16), w_out[0].astype(BF16), tm=512)

    tf = 256
    nf = d_ff // tf
    wup = w_up[0].astype(BF16)
    wa = wup[:, :d_ff].reshape(d, nf, tf).transpose(1, 0, 2)
    wu = wup[:, d_ff:].reshape(d, nf, tf).transpose(1, 0, 2)
    cw = conv_ffn[0].reshape(9, nf, tf).transpose(1, 0, 2)
    wd = w_down[0].astype(BF16).reshape(nf, tf, d)
    return _ffn(x1, mod3, norm_pre_ffn, norm_post_ffn, wa, wu, cw, wd, tm=512)
```

```python
import functools
import math

import numpy as np
import jax
import jax.numpy as jnp
from jax import lax
from jax.experimental import pallas as pl
from jax.experimental.pallas import tpu as pltpu

F32 = jnp.float32
BF16 = jnp.bfloat16

GRID_W = 64
N_GROUPS = 4
GROUP_DIM = 128
F_WIDTH = N_GROUPS * GROUP_DIM
HEADS = 8
HEAD_DIM = 128
DN_WIDTH = HEADS * HEAD_DIM
N_DIR = 2
NORM_EPS = 1e-6
L2_EPS = 1e-6
CHUNK = 128
LOG2_CHUNK = 7
FFT_N1 = 8
VMEM_LIMIT = 56 * 1024 * 1024


def _cparams(*sem):
    return pltpu.CompilerParams(dimension_semantics=sem, vmem_limit_bytes=VMEM_LIMIT)


def _sigmoid(x):
    return 1.0 / (1.0 + jnp.exp(-x))


def _silu(x):
    return x * _sigmoid(x)


def _softplus(x):
    return jnp.maximum(x, 0.0) + jnp.log1p(jnp.exp(-jnp.abs(x)))


def _dot(a, b):
    return jnp.dot(a, b, preferred_element_type=F32)


def _dot_nt(a, b):
    return lax.dot_general(a, b, (((1,), (1,)), ((), ())), preferred_element_type=F32)


def _split3(x):
    x1 = x.astype(BF16)
    r1 = x - x1.astype(F32)
    x2 = r1.astype(BF16)
    x3 = (r1 - x2.astype(F32)).astype(BF16)
    return x1, x2, x3


def _resident(shape):
    nd = len(shape)
    return pl.BlockSpec(shape, lambda *_: (0,) * nd, pipeline_mode=pl.Buffered(1))


def _ada_kernel(c_ref, w_ref, b_ref, o_ref):
    s = _silu(c_ref[...])
    o_ref[...] = jnp.dot(s, w_ref[...], preferred_element_type=F32,
                         precision=lax.Precision.HIGHEST) + b_ref[...]


def _ada(c_rows, w_ada, b_ada):
    rows, d = c_rows.shape
    n = w_ada.shape[1]
    tn = n // 4
    return pl.pallas_call(
        _ada_kernel,
        grid=(n // tn,),
        in_specs=[pl.BlockSpec((rows, d), lambda j: (0, 0)),
                  pl.BlockSpec((d, tn), lambda j: (0, j)),
                  pl.BlockSpec((1, tn), lambda j: (0, j))],
        out_specs=pl.BlockSpec((rows, tn), lambda j: (0, j)),
        out_shape=jax.ShapeDtypeStruct((rows, n), F32),
        compiler_params=_cparams("arbitrary"),
        name="ada",
    )(c_rows, w_ada, b_ada)


def _norm_mod(x, nw, sh, sc):
    ms = jnp.mean(x * x, axis=-1, keepdims=True)
    y = x * lax.rsqrt(ms + NORM_EPS) * nw
    return y * (1.0 + sc) + sh


def _inproj_kernel(x_ref, sh_ref, sc_ref, nw_ref, w_ref, wba_ref, p_ref, ba_ref, hb_ref, *, tn, rows):
    tm = x_ref.shape[0]
    for r0 in range(0, tm, rows):
        h = _norm_mod(x_ref[r0:r0 + rows, :], nw_ref[...], sh_ref[...], sc_ref[...])
        hb_ref[r0:r0 + rows, :] = h.astype(BF16)
    hb = hb_ref[...]
    for j in range(w_ref.shape[1] // tn):
        p_ref[:, j * tn:(j + 1) * tn] = _dot(hb, w_ref[:, j * tn:(j + 1) * tn]).astype(BF16)
    ba_ref[...] = _dot(hb, wba_ref[...])


def _inproj(x, mod3, mod_row, nw, w_main, w_ba, tm):
    b, t, d = x.shape
    nc = w_main.shape[1]
    kern = functools.partial(_inproj_kernel, tn=512, rows=128)
    return pl.pallas_call(
        kern,
        grid=(b, t // tm),
        in_specs=[pl.BlockSpec((None, tm, d), lambda i, j: (i, j, 0)),
                  pl.BlockSpec((None, 1, d), lambda i, j: (mod_row(i), 0, 0)),
                  pl.BlockSpec((None, 1, d), lambda i, j: (mod_row(i), 0, 1)),
                  pl.BlockSpec((1, d), lambda i, j: (0, 0)),
                  _resident((d, nc)),
                  _resident((d, 128))],
        out_specs=[pl.BlockSpec((None, tm, nc), lambda i, j: (i, j, 0)),
                   pl.BlockSpec((None, tm, 128), lambda i, j: (i, j, 0))],
        out_shape=[jax.ShapeDtypeStruct((b, t, nc), BF16),
                   jax.ShapeDtypeStruct((b, t, 128), F32)],
        scratch_shapes=[pltpu.VMEM((tm, d), BF16)],
        compiler_params=_cparams("arbitrary", "arbitrary"),
        name="inproj",
    )(x, mod3, mod3, nw, w_main, w_ba)


PCOL_G0 = 0
PCOL_G1 = 8
PCOL_Z = 16
PCOL_Q = 24
PCOL_K = 32
PCOL_V = 40
PCOL_F = 48
P_WIDTH = 52 * 128


def _conv_block(x_ref, w_ref, r, nblk):
    t = x_ref.shape[0]
    r0 = pl.multiple_of(r * CHUNK, CHUNK)
    main = x_ref[pl.ds(r0, CHUNK), :].astype(F32)
    p0 = pl.multiple_of(jnp.maximum(r0 - 16, 0), 16)
    n0 = pl.multiple_of(jnp.minimum(r0 + CHUNK, t - 16), 16)
    prev = x_ref[pl.ds(p0, 16), :].astype(F32)[15:16, :]
    nxt = x_ref[pl.ds(n0, 16), :].astype(F32)[0:1, :]
    prev = jnp.where(r > 0, prev, 0.0)
    nxt = jnp.where(r < nblk - 1, nxt, 0.0)
    row = lax.broadcasted_iota(jnp.int32, (CHUNK, HEAD_DIM), 0)
    dn = jnp.where(row == 0, prev, pltpu.roll(main, 1, 0))
    up = jnp.where(row == CHUNK - 1, nxt, pltpu.roll(main, CHUNK - 1, 0))
    w = w_ref[...]
    y = w[0:1, :] * dn + w[1:2, :] * main + w[2:3, :] * up
    return _silu(y)


def _l2norm(y):
    return y * lax.rsqrt(jnp.sum(y * y, axis=-1, keepdims=True) + L2_EPS)


def _deltanet_kernel(q_ref, k_ref, v_ref, z_ref, ba_ref, wq_ref, wk_ref, wv_ref, alog_ref, dtb_ref,
                     dnw_ref, s0_ref, *rest, emit_o, unroll):
    if emit_o:
        o_ref, sout_ref = rest[0], rest[1]
        mq_ref, n_ref, oacc_ref = rest[2:]
    else:
        sout_ref = rest[0]
        mq_ref, n_ref = rest[1:]
    t = q_ref.shape[0]
    nblk = t // CHUNK
    h = pl.program_id(1)

    row_i = lax.broadcasted_iota(jnp.int32, (CHUNK, CHUNK), 0)
    col_i = lax.broadcasted_iota(jnp.int32, (CHUNK, CHUNK), 1)
    lane = col_i
    tri_lo = (row_i >= col_i).astype(BF16)
    a_neg = -jnp.exp(alog_ref[...])
    dtb = dtb_ref[...]
    xor = row_i ^ col_i
    lvl = jnp.zeros((CHUNK, CHUNK), jnp.int32)
    for l in range(1, LOG2_CHUNK):
        lvl = jnp.where(xor >= (1 << l), l, lvl)
    eye = (row_i == col_i).astype(F32)
    sub8 = lax.broadcasted_iota(jnp.int32, (HEADS, CHUNK), 0)

    def block_terms(blk):
        r0 = pl.multiple_of(blk * CHUNK, CHUNK)
        q = _l2norm(_conv_block(q_ref, wq_ref, blk, nblk)) * (HEAD_DIM ** -0.5)
        k = _l2norm(_conv_block(k_ref, wk_ref, blk, nblk))
        v = _conv_block(v_ref, wv_ref, blk, nblk)
        raw = ba_ref[pl.ds(r0, CHUNK), :]
        gate = jnp.where(lane < 16, _sigmoid(raw), a_neg * _softplus(raw + dtb))
        return dict(blk=blk, r0=r0, q=q, k=k, v=v, kt=k.T, kb=k.astype(BF16), gate=gate)

    def chain_setup(bt, d):
        gc, gct, kk = bt["gc"], bt["gct"], bt["kk"]
        beta = jnp.sum(jnp.where(lane == 8 * d + h, gc, 0.0), axis=1, keepdims=True)
        gcol = jnp.sum(jnp.where(lane == 16 + 8 * d + h, gc, 0.0), axis=1, keepdims=True)
        grow = jnp.sum(jnp.where(sub8 == h, gct[16 + 8 * d:24 + 8 * d, :], 0.0),
                       axis=0, keepdims=True)
        if d == 0:
            incl, strict = row_i >= col_i, row_i > col_i
            glast = gcol[CHUNK - 1:CHUNK, :]
        else:
            incl, strict = row_i <= col_i, row_i < col_i
            glast = gcol[0:1, :]
        decay = jnp.exp(jnp.where(incl, gcol - grow, -jnp.inf))
        lmat = jnp.where(strict, beta * kk * decay, 0.0)
        egc = jnp.exp(gcol)
        rhs = jnp.concatenate([bt["k"] * (beta * egc), bt["v"] * beta], axis=1).astype(BF16)
        ktail = (bt["kt"] * jnp.exp(glast - grow)).astype(BF16)
        return dict(bt=bt, d=d, decay=decay, lmat=lmat, egc=egc, rhs=rhs, ktail=ktail, glast=glast,
                    tinv=eye - jnp.where(lvl == 0, lmat, 0.0))

    def pass_b(i, states):
        new = []
        for d in range(N_DIR):
            blk = i if d == 0 else nblk - 1 - i
            r = _dot(mq_ref[d, blk], states[d].astype(BF16))
            if emit_o:
                r0 = pl.multiple_of(blk * CHUNK, CHUNK)
                oacc_ref[d, pl.ds(r0, CHUNK), :] += r[CHUNK:]
            new.append(r[:CHUNK] + n_ref[d, blk])
        return tuple(new)

    half = max(unroll // 2, 1)

    def pass_a(i, states, with_b):
        pending = [(i - 1) * half + u for u in range(half)] if with_b else []

        def recurrence_step(st):
            return pass_b(pending.pop(0), st) if pending else st

        ids = ([i * half + u for u in range(half)] + [nblk - 1 - (i * half + u) for u in range(half)]
               if unroll > 1 else [i])
        blocks = [block_terms(b) for b in ids]
        for bt in blocks:
            bt["g3"] = _split3(bt["gate"])
        for bt in blocks:
            bt["pre"] = sum(_dot(tri_lo, g) for g in bt["g3"])
            bt["suf"] = bt["pre"][CHUNK - 1:CHUNK, :] - bt["pre"] + bt["gate"]
        states = recurrence_step(states)
        for bt in blocks:
            bt["kk"] = _dot_nt(bt["kb"], bt["kb"])
            if emit_o:
                bt["qkt"] = _dot_nt(bt["q"].astype(BF16), bt["kb"])
        for bt in blocks:
            bt["gc"] = jnp.where(lane < 16, bt["gate"], jnp.where(lane < 24, bt["pre"], bt["suf"]))
            bt["gct"] = bt["gc"].T
        chains = [chain_setup(bt, d) for bt in blocks for d in range(N_DIR)]
        for l in range(1, LOG2_CHUNK):
            for c in chains:
                c["tb"] = c["tinv"].astype(BF16)
                c["x"] = _dot(c["tb"], jnp.where(lvl == l, c["lmat"], 0.0).astype(BF16)).astype(BF16)
            for c in chains:
                c["tinv"] = c["tinv"] - _dot(c["x"], c["tb"])
            states = recurrence_step(states)
        for c in chains:
            c["wu"] = _dot(c["tinv"].astype(BF16), c["rhs"]).astype(BF16)
        for c in chains:
            c["ku"] = _dot(c["ktail"], c["wu"])
            if emit_o:
                c["qu"] = _dot((c["bt"]["qkt"] * c["decay"]).astype(BF16), c["wu"])
        for c in chains:
            d, bt = c["d"], c["bt"]
            m = eye * jnp.exp(c["glast"]) - c["ku"][:, :HEAD_DIM]
            n_ref[d, bt["blk"]] = c["ku"][:, HEAD_DIM:]
            if emit_o:
                qe = bt["q"] * c["egc"] - c["qu"][:, :HEAD_DIM]
                mq_ref[d, bt["blk"]] = jnp.concatenate([m, qe], axis=0).astype(BF16)
                oacc_ref[d, pl.ds(bt["r0"], CHUNK), :] = c["qu"][:, HEAD_DIM:]
            else:
                mq_ref[d, bt["blk"]] = m.astype(BF16)
        while pending:
            states = recurrence_step(states)
        return states

    n_iter = nblk // unroll
    states = (s0_ref[0], s0_ref[1])
    if unroll > 1:
        states = pass_a(0, states, False)
        states = lax.fori_loop(1, n_iter, lambda i, st: pass_a(i, st, True), states)
        done = (n_iter - 1) * half
    else:
        states = lax.fori_loop(0, n_iter, lambda i, st: pass_a(i, st, False), states)
        done = 0
    states = lax.fori_loop(done, nblk, pass_b, states, unroll=2)
    sout_ref[0] = states[0]
    sout_ref[1] = states[1]

    if emit_o:
        def fin(r, carry):
            r0 = pl.multiple_of(r * CHUNK, CHUNK)
            o = oacc_ref[0, pl.ds(r0, CHUNK), :] + oacc_ref[1, pl.ds(r0, CHUNK), :]
            on = o * lax.rsqrt(jnp.mean(o * o, axis=-1, keepdims=True) + NORM_EPS) * dnw_ref[...]
            z = z_ref[pl.ds(r0, CHUNK), :].astype(F32)
            o_ref[pl.ds(r0, CHUNK), :] = (on * _silu(z)).astype(BF16)
            return carry

        lax.fori_loop(0, nblk, fin, 0, unroll=unroll)


def _deltanet(p, ba, convw, alog_row, dtb_row, dnw, s0, emit_o):
    b, t, _ = p.shape
    nblk = t // CHUNK
    unroll = 8 if nblk % 8 == 0 else (2 if nblk % 2 == 0 else 1)
    tok = lambda c0: pl.BlockSpec((None, t, 128), lambda i, j: (i, 0, c0 + j))
    cw = lambda c0: pl.BlockSpec((3, 128), lambda i, j: (0, c0 + j))
    par = pl.BlockSpec((1, 128), lambda i, j: (0, 0))
    st = pl.BlockSpec((N_DIR, None, None, HEAD_DIM, HEAD_DIM), lambda i, j: (0, i, j, 0, 0))
    out_specs = [st]
    out_shape = [jax.ShapeDtypeStruct((N_DIR, b, HEADS, HEAD_DIM, HEAD_DIM), F32)]
    scratch = [pltpu.VMEM((N_DIR, nblk, (2 if emit_o else 1) * CHUNK, HEAD_DIM), BF16),
               pltpu.VMEM((N_DIR, nblk, HEAD_DIM, HEAD_DIM), F32)]
    if emit_o:
        out_specs = [pl.BlockSpec((None, t, 128), lambda i, j: (i, 0, j))] + out_specs
        out_shape = [jax.ShapeDtypeStruct((b, t, DN_WIDTH), BF16)] + out_shape
        scratch.append(pltpu.VMEM((N_DIR, t, HEAD_DIM), F32))
    return pl.pallas_call(
        functools.partial(_deltanet_kernel, emit_o=emit_o, unroll=unroll),
        grid=(b, HEADS),
        in_specs=[tok(PCOL_Q), tok(PCOL_K), tok(PCOL_V), tok(PCOL_Z),
                  pl.BlockSpec((None, t, 128), lambda i, j: (i, 0, 0)),
                  cw(0), cw(HEADS), cw(2 * HEADS), par, par, par, st],
        out_specs=out_specs,
        out_shape=out_shape,
        scratch_shapes=scratch,
        compiler_params=_cparams("arbitrary", "arbitrary"),
        name="deltanet_x" if emit_o else "deltanet_ctx",
    )(p, p, p, p, ba, convw, convw, convw, alog_row, dtb_row, dnw, s0)


def _fourier_kernel(x_ref, wc_ref, twc_ref, tws_ref, cn_ref, sn_ref, o_ref, y_ref, a_ref, *, rows):
    n1 = FFT_N1
    t = x_ref.shape[0]
    n2 = t // n1
    fw = F_WIDTH
    wc = wc_ref[...].astype(BF16)
    for t1 in range(n1):
        y_ref[t1] = _dot(x_ref[t1 * n2:(t1 + 1) * n2, :], wc).astype(BF16)

    ang = [2.0 * math.pi * m / n1 for m in range(n1)]
    snap = lambda val: float(round(val)) if abs(val - round(val)) < 1e-9 else val
    cs = [(snap(math.cos(a)), snap(math.sin(a))) for a in ang]

    def axpy(acc, coef, val):
        if coef == 0.0:
            return acc
        if acc is None:
            return val if coef == 1.0 else (-val if coef == -1.0 else coef * val)
        if coef == 1.0:
            return acc + val
        if coef == -1.0:
            return acc - val
        return acc + coef * val

    hw = 256

    def stage1(rb, carry):
        r0 = pl.multiple_of(rb * rows, rows)
        for l0 in range(0, fw, hw):
            yr = [y_ref[t1, pl.ds(r0, rows), l0:l0 + hw].astype(F32) for t1 in range(n1)]
            yi = [y_ref[t1, pl.ds(r0, rows), fw + l0:fw + l0 + hw].astype(F32) for t1 in range(n1)]
            for k1 in range(n1):
                ar = None
                ai = None
                for t1 in range(n1):
                    c, s = cs[(t1 * k1) % n1]
                    ar = axpy(axpy(ar, c, yr[t1]), s, yi[t1])
                    ai = axpy(axpy(ai, c, yi[t1]), -s, yr[t1])
                twc = jnp.concatenate([twc_ref[k1, pl.ds(r0, rows), :]] * (hw // 128), axis=1)
                tws = jnp.concatenate([tws_ref[k1, pl.ds(r0, rows), :]] * (hw // 128), axis=1)
                a_ref[k1, 0, pl.ds(r0, rows), l0:l0 + hw] = (ar * twc + ai * tws).astype(BF16)
                a_ref[k1, 1, pl.ds(r0, rows), l0:l0 + hw] = (ai * twc - ar * tws).astype(BF16)
        return carry

    lax.fori_loop(0, n2 // rows, stage1, 0)

    cn = cn_ref[...].astype(BF16)
    sn = sn_ref[...].astype(BF16)
    for k1 in range(n1):
        zr = _dot(cn, a_ref[k1, 0]) + _dot(sn, a_ref[k1, 1])
        for g in range(N_GROUPS):
            o_ref[g, pl.ds(k1, n2, stride=n1), :] = zr[:, g * GROUP_DIM:(g + 1) * GROUP_DIM]


def _fourier_consts(t):
    n1 = FFT_N1
    n2 = t // n1
    scale = 1.0 / math.sqrt(t * GROUP_DIM)
    j = np.arange(GROUP_DIM)
    ang = 2.0 * np.pi * np.outer(j, j) / GROUP_DIM
    wc = np.zeros((F_WIDTH, 2 * F_WIDTH), np.float32)
    for g in range(N_GROUPS):
        sl = slice(g * GROUP_DIM, (g + 1) * GROUP_DIM)
        wc[sl, sl] = np.cos(ang) * scale
        wc[sl, F_WIDTH + g * GROUP_DIM:F_WIDTH + (g + 1) * GROUP_DIM] = -np.sin(ang) * scale
    t2 = np.arange(n2)
    k1 = np.arange(n1)
    tw = 2.0 * np.pi * np.outer(k1, t2) / t
    a2 = 2.0 * np.pi * np.outer(t2, t2) / n2
    return (wc, np.cos(tw).astype(np.float32), np.sin(tw).astype(np.float32),
            np.cos(a2).astype(np.float32), np.sin(a2).astype(np.float32))


def _fourier(p):
    b, t, _ = p.shape
    n1 = FFT_N1
    n2 = t // n1
    wc, twc, tws, cn, sn = _fourier_consts(t)
    wc, cn, sn = jnp.asarray(wc), jnp.asarray(cn), jnp.asarray(sn)
    twc = jnp.broadcast_to(jnp.asarray(twc)[:, :, None], (n1, n2, 128))
    tws = jnp.broadcast_to(jnp.asarray(tws)[:, :, None], (n1, n2, 128))
    return pl.pallas_call(
        functools.partial(_fourier_kernel, rows=16),
        grid=(b,),
        in_specs=[pl.BlockSpec((None, t, F_WIDTH), lambda i: (i, 0, PCOL_F // 4)),
                  _resident((F_WIDTH, 2 * F_WIDTH)),
                  _resident((n1, n2, 128)), _resident((n1, n2, 128)),
                  _resident((n2, n2)), _resident((n2, n2))],
        out_specs=pl.BlockSpec((None, N_GROUPS, t, GROUP_DIM), lambda i: (i, 0, 0, 0)),
        out_shape=jax.ShapeDtypeStruct((b, N_GROUPS, t, GROUP_DIM), F32),
        scratch_shapes=[pltpu.VMEM((n1, n2, 2 * F_WIDTH), BF16),
                        pltpu.VMEM((n1, 2, n2, F_WIDTH), BF16)],
        compiler_params=_cparams("arbitrary"),
        name="fourier",
    )(p, wc, twc, tws, cn, sn)


def _merge_kernel(x_ref, g0_ref, g1_ref, o_ref, f_ref, gt_ref, nw_ref, wf_ref, wd_ref, wo_ref, out_ref,
                  *, rows):
    tm = x_ref.shape[0]
    for r0 in range(0, tm, rows):
        sl = slice(r0, r0 + rows)
        fo = jnp.concatenate([f_ref[g, sl, :] for g in range(N_GROUPS)], axis=1).astype(BF16)
        yf = _dot(fo, wf_ref[...])
        yd = _dot(o_ref[sl, :], wd_ref[...])
        m = _sigmoid(g0_ref[sl, :].astype(F32)) * yf + _sigmoid(g1_ref[sl, :].astype(F32)) * yd
        y = _dot(m.astype(BF16), wo_ref[...])
        yn = y * lax.rsqrt(jnp.mean(y * y, axis=-1, keepdims=True) + NORM_EPS) * nw_ref[...]
        out_ref[sl, :] = x_ref[sl, :] + gt_ref[...] * yn


def _merge(x, p, o, four, mod3, nw, wf, wd, wo, tm):
    b, t, d = x.shape
    return pl.pallas_call(
        functools.partial(_merge_kernel, rows=256),
        grid=(b, t // tm),
        in_specs=[pl.BlockSpec((None, tm, d), lambda i, j: (i, j, 0)),
                  pl.BlockSpec((None, tm, d), lambda i, j: (i, j, PCOL_G0 // 8)),
                  pl.BlockSpec((None, tm, d), lambda i, j: (i, j, PCOL_G1 // 8)),
                  pl.BlockSpec((None, tm, DN_WIDTH), lambda i, j: (i, j, 0)),
                  pl.BlockSpec((None, N_GROUPS, tm, GROUP_DIM), lambda i, j: (i, 0, j, 0)),
                  pl.BlockSpec((None, 1, d), lambda i, j: (i, 0, 2)),
                  pl.BlockSpec((1, d), lambda i, j: (0, 0)),
                  _resident(wf.shape), _resident(wd.shape), _resident(wo.shape)],
        out_specs=pl.BlockSpec((None, tm, d), lambda i, j: (i, j, 0)),
        out_shape=jax.ShapeDtypeStruct((b, t, d), F32),
        compiler_params=_cparams("arbitrary", "arbitrary"),
        name="merge",
    )(x, p, p, o, four, mod3, nw, wf, wd, wo)


def _ffn_kernel(x_ref, xp_ref, xn_ref, sh_ref, sc_ref, gt_ref, npre_ref, npost_ref,
                wa_ref, wu_ref, cw_ref, wd_ref, out_ref, hb_ref, a_ref, a2_ref, acc_ref):
    tm = x_ref.shape[0]
    gw = GRID_W
    i = pl.program_id(1)
    last = pl.num_programs(1) - 1
    nf = wa_ref.shape[0]

    def nm(xt):
        return _norm_mod(xt, npre_ref[...], sh_ref[...], sc_ref[...]).astype(BF16)

    hb_ref[0:gw, :] = nm(xp_ref[...])
    for r0 in range(0, tm, 128):
        hb_ref[gw + r0:gw + r0 + 128, :] = nm(x_ref[r0:r0 + 128, :])
    hb_ref[gw + tm:, :] = nm(xn_ref[...])
    acc_ref[...] = jnp.zeros_like(acc_ref)
    top = jnp.where(i > 0, 1.0, 0.0)
    bot = jnp.where(i < last, 1.0, 0.0)
    colid = lax.broadcasted_iota(jnp.int32, (tm, 1), 0) % gw
    has_l = colid != 0
    has_r = colid != gw - 1

    def up_a(dst_ref, fj):
        a = _dot(hb_ref[...], wa_ref[fj])
        dst_ref[0:gw, :] = a[0:gw] * top
        dst_ref[gw:gw + tm, :] = a[gw:gw + tm]
        dst_ref[gw + tm:, :] = a[gw + tm:] * bot

    def consume(src_ref, fj):
        cw = cw_ref[fj]
        s0 = src_ref[0:tm, :]
        s1 = src_ref[gw:gw + tm, :]
        s2 = src_ref[2 * gw:2 * gw + tm, :]
        left = cw[0:1] * s0 + cw[3:4] * s1 + cw[6:7] * s2
        mid = cw[1:2] * s0 + cw[4:5] * s1 + cw[7:8] * s2
        right = cw[2:3] * s0 + cw[5:6] * s1 + cw[8:9] * s2
        conv = (mid + jnp.where(has_l, pltpu.roll(left, 1, 0), 0.0)
                + jnp.where(has_r, pltpu.roll(right, tm - 1, 0), 0.0))
        u = _dot(hb_ref[gw:gw + tm, :], wu_ref[fj])
        act = (_silu(conv) * u).astype(BF16)
        acc_ref[...] += _dot(act, wd_ref[fj])

    up_a(a_ref, 0)

    def body(i, carry):
        f0 = 2 * i
        up_a(a2_ref, f0 + 1)
        consume(a_ref, f0)
        up_a(a_ref, jnp.minimum(f0 + 2, nf - 1))
        consume(a2_ref, f0 + 1)
        return carry

    lax.fori_loop(0, nf // 2, body, 0)
    if nf % 2 == 1:
        consume(a_ref, nf - 1)
    for r0 in range(0, tm, 128):
        sl = slice(r0, r0 + 128)
        y = acc_ref[sl, :]
        yn = y * lax.rsqrt(jnp.mean(y * y, axis=-1, keepdims=True) + NORM_EPS) * npost_ref[...]
        out_ref[sl, :] = x_ref[sl, :] + gt_ref[...] * yn


def _ffn(x, mod3, npre, npost, wa, wu, cw, wd, tm):
    b, t, d = x.shape
    gw = GRID_W
    rpt = tm // gw
    nrow = t // gw
    nf, _, tf = wa.shape
    return pl.pallas_call(
        _ffn_kernel,
        grid=(b, t // tm),
        in_specs=[pl.BlockSpec((None, tm, d), lambda i, j: (i, j, 0)),
                  pl.BlockSpec((None, gw, d), lambda i, j: (i, jnp.maximum(j * rpt - 1, 0), 0)),
                  pl.BlockSpec((None, gw, d), lambda i, j: (i, jnp.minimum(j * rpt + rpt, nrow - 1), 0)),
                  pl.BlockSpec((None, 1, d), lambda i, j: (i, 0, 3)),
                  pl.BlockSpec((None, 1, d), lambda i, j: (i, 0, 4)),
                  pl.BlockSpec((None, 1, d), lambda i, j: (i, 0, 5)),
                  pl.BlockSpec((1, d), lambda i, j: (0, 0)),
                  pl.BlockSpec((1, d), lambda i, j: (0, 0)),
                  _resident(wa.shape), _resident(wu.shape), _resident(cw.shape), _resident(wd.shape)],
        out_specs=pl.BlockSpec((None, tm, d), lambda i, j: (i, j, 0)),
        out_shape=jax.ShapeDtypeStruct((b, t, d), F32),
        scratch_shapes=[pltpu.VMEM((tm + 2 * gw, d), BF16),
                        pltpu.VMEM((tm + 2 * gw, tf), F32),
                        pltpu.VMEM((tm + 2 * gw, tf), F32),
                        pltpu.VMEM((tm, d), F32)],
        compiler_params=_cparams("arbitrary", "arbitrary"),
        name="ffn",
    )(x, x, x, mod3, mod3, mod3, npre, npost, wa, wu, cw, wd)


def kernel(x, c, ctx, c_ctx, w_ada, b_ada, norm_pre_mix, norm_post_mix, norm_pre_ffn, norm_post_ffn,
           w_in, conv_qkv, a_log, dt_bias, dn_norm, w_fourier, w_dn, w_out, w_up, conv_ffn, w_down):
    assert w_ada.shape[0] == 1, "single-layer stack"
    bsz, seq, d = x.shape
    d_ff = w_down.shape[1]
    assert d == DN_WIDTH and seq % (FFT_N1 * 16) == 0 and seq % 512 == 0 and ctx.shape[1] % CHUNK == 0

    pad = (-(bsz + 1)) % 8
    c_rows = jnp.concatenate([c, c_ctx[None, :], jnp.zeros((pad, d), F32)], axis=0)
    mod = _ada(c_rows, w_ada[0], b_ada)
    mod3 = mod.reshape(mod.shape[0], 1, 6 * d)

    wi = w_in[0]
    o_q = F_WIDTH
    o_z = o_q + 3 * DN_WIDTH
    o_b = o_z + DN_WIDTH
    o_g = o_b + 2 * N_DIR * HEADS
    w_main = jnp.concatenate([wi[:, o_g:], wi[:, o_z:o_b], wi[:, o_q:o_z], wi[:, :o_q]], axis=1).astype(BF16)
    w_ba = jnp.pad(wi[:, o_b:o_g], ((0, 0), (0, 128 - 2 * N_DIR * HEADS))).astype(BF16)
    npm = norm_pre_mix
    p_c, ba_c = _inproj(ctx, mod3, lambda i: bsz, npm, w_main, w_ba, tm=ctx.shape[1])
    p_x, ba_x = _inproj(x, mod3, lambda i: i, npm, w_main, w_ba, tm=512)

    lanes = jnp.zeros((1, 128), F32)
    alog_row = lanes.at[0, 16:32].set(a_log[0].reshape(-1))
    dtb_row = lanes.at[0, 16:32].set(dt_bias[0].reshape(-1))
    zero = jnp.zeros((N_DIR, bsz, HEADS, HEAD_DIM, HEAD_DIM), F32)
    (s_ctx,) = _deltanet(p_c, ba_c, conv_qkv[0], alog_row, dtb_row, dn_norm, zero, emit_o=False)
    o_x, _ = _deltanet(p_x, ba_x, conv_qkv[0], alog_row, dtb_row, dn_norm, s_ctx, emit_o=True)

    four = _fourier(p_x)
    x1 = _merge(x, p_x, o_x, four, mod3, norm_post_mix, w_fourier[0].astype(BF16),
                w_dn[0].astype(BF16), w_out[0].astype(BF16), tm=512)

    tf = 256
    nf = d_ff // tf
    wup = w_up[0].astype(BF16)
    wa = wup[:, :d_ff].reshape(d, nf, tf).transpose(1, 0, 2)
    wu = wup[:, d_ff:].reshape(d, nf, tf).transpose(1, 0, 2)
    cw = conv_ffn[0].reshape(9, nf, tf).transpose(1, 0, 2)
    wd = w_down[0].astype(BF16).reshape(nf, tf, d)
    return _ffn(x1, mod3, norm_pre_ffn, norm_post_ffn, wa, wu, cw, wd, tm=512)
```

```python
import functools
import math

import numpy as np
import jax
import jax.numpy as jnp
from jax import lax
from jax.experimental import pallas as pl
from jax.experimental.pallas import tpu as pltpu

F32 = jnp.float32
BF16 = jnp.bfloat16

GRID_W = 64
N_GROUPS = 4
GROUP_DIM = 128
F_WIDTH = N_GROUPS * GROUP_DIM
HEADS = 8
HEAD_DIM = 128
DN_WIDTH = HEADS * HEAD_DIM
N_DIR = 2
NORM_EPS = 1e-6
L2_EPS = 1e-6
CHUNK = 128
LOG2_CHUNK = 7
FFT_N1 = 8
VMEM_LIMIT = 56 * 1024 * 1024


def _cparams(*sem):
    return pltpu.CompilerParams(dimension_semantics=sem, vmem_limit_bytes=VMEM_LIMIT)


def _sigmoid(x):
    return 1.0 / (1.0 + jnp.exp(-x))


def _silu(x):
    return x * _sigmoid(x)


def _softplus(x):
    return jnp.maximum(x, 0.0) + jnp.log1p(jnp.exp(-jnp.abs(x)))


def _dot(a, b):
    return jnp.dot(a, b, preferred_element_type=F32)


def _dot_nt(a, b):
    return lax.dot_general(a, b, (((1,), (1,)), ((), ())), preferred_element_type=F32)


def _split3(x):
    x1 = x.astype(BF16)
    r1 = x - x1.astype(F32)
    x2 = r1.astype(BF16)
    x3 = (r1 - x2.astype(F32)).astype(BF16)
    return x1, x2, x3


def _resident(shape):
    nd = len(shape)
    return pl.BlockSpec(shape, lambda *_: (0,) * nd, pipeline_mode=pl.Buffered(1))


def _ada_kernel(c_ref, w_ref, b_ref, o_ref):
    s = _silu(c_ref[...])
    o_ref[...] = jnp.dot(s, w_ref[...], preferred_element_type=F32,
                         precision=lax.Precision.HIGHEST) + b_ref[...]


def _ada(c_rows, w_ada, b_ada):
    rows, d = c_rows.shape
    n = w_ada.shape[1]
    tn = n // 4
    return pl.pallas_call(
        _ada_kernel,
        grid=(n // tn,),
        in_specs=[pl.BlockSpec((rows, d), lambda j: (0, 0)),
                  pl.BlockSpec((d, tn), lambda j: (0, j)),
                  pl.BlockSpec((1, tn), lambda j: (0, j))],
        out_specs=pl.BlockSpec((rows, tn), lambda j: (0, j)),
        out_shape=jax.ShapeDtypeStruct((rows, n), F32),
        compiler_params=_cparams("arbitrary"),
        name="ada",
    )(c_rows, w_ada, b_ada)


def _norm_mod(x, nw, sh, sc):
    ms = jnp.mean(x * x, axis=-1, keepdims=True)
    y = x * lax.rsqrt(ms + NORM_EPS) * nw
    return y * (1.0 + sc) + sh


def _inproj_kernel(x_ref, sh_ref, sc_ref, nw_ref, w_ref, wba_ref, p_ref, ba_ref, hb_ref, *, tn, rows):
    tm = x_ref.shape[0]
    for r0 in range(0, tm, rows):
        h = _norm_mod(x_ref[r0:r0 + rows, :], nw_ref[...], sh_ref[...], sc_ref[...])
        hb_ref[r0:r0 + rows, :] = h.astype(BF16)
    hb = hb_ref[...]
    for j in range(w_ref.shape[1] // tn):
        p_ref[:, j * tn:(j + 1) * tn] = _dot(hb, w_ref[:, j * tn:(j + 1) * tn]).astype(BF16)
    ba_ref[...] = _dot(hb, wba_ref[...])


def _inproj(x, mod3, mod_row, nw, w_main, w_ba, tm):
    b, t, d = x.shape
    nc = w_main.shape[1]
    kern = functools.partial(_inproj_kernel, tn=512, rows=128)
    return pl.pallas_call(
        kern,
        grid=(b, t // tm),
        in_specs=[pl.BlockSpec((None, tm, d), lambda i, j: (i, j, 0)),
                  pl.BlockSpec((None, 1, d), lambda i, j: (mod_row(i), 0, 0)),
                  pl.BlockSpec((None, 1, d), lambda i, j: (mod_row(i), 0, 1)),
                  pl.BlockSpec((1, d), lambda i, j: (0, 0)),
                  _resident((d, nc)),
                  _resident((d, 128))],
        out_specs=[pl.BlockSpec((None, tm, nc), lambda i, j: (i, j, 0)),
                   pl.BlockSpec((None, tm, 128), lambda i, j: (i, j, 0))],
        out_shape=[jax.ShapeDtypeStruct((b, t, nc), BF16),
                   jax.ShapeDtypeStruct((b, t, 128), F32)],
        scratch_shapes=[pltpu.VMEM((tm, d), BF16)],
        compiler_params=_cparams("arbitrary", "arbitrary"),
        name="inproj",
    )(x, mod3, mod3, nw, w_main, w_ba)


PCOL_G0 = 0
PCOL_G1 = 8
PCOL_Z = 16
PCOL_Q = 24
PCOL_K = 32
PCOL_V = 40
PCOL_F = 48
P_WIDTH = 52 * 128


def _conv_block(x_ref, w_ref, r, nblk):
    t = x_ref.shape[0]
    r0 = pl.multiple_of(r * CHUNK, CHUNK)
    main = x_ref[pl.ds(r0, CHUNK), :].astype(F32)
    p0 = pl.multiple_of(jnp.maximum(r0 - 16, 0), 16)
    n0 = pl.multiple_of(jnp.minimum(r0 + CHUNK, t - 16), 16)
    prev = x_ref[pl.ds(p0, 16), :].astype(F32)[15:16, :]
    nxt = x_ref[pl.ds(n0, 16), :].astype(F32)[0:1, :]
    prev = jnp.where(r > 0, prev, 0.0)
    nxt = jnp.where(r < nblk - 1, nxt, 0.0)
    row = lax.broadcasted_iota(jnp.int32, (CHUNK, HEAD_DIM), 0)
    dn = jnp.where(row == 0, prev, pltpu.roll(main, 1, 0))
    up = jnp.where(row == CHUNK - 1, nxt, pltpu.roll(main, CHUNK - 1, 0))
    w = w_ref[...]
    y = w[0:1, :] * dn + w[1:2, :] * main + w[2:3, :] * up
    return _silu(y)


def _l2norm(y):
    return y * lax.rsqrt(jnp.sum(y * y, axis=-1, keepdims=True) + L2_EPS)


def _deltanet_kernel(q_ref, k_ref, v_ref, z_ref, ba_ref, wq_ref, wk_ref, wv_ref, alog_ref, dtb_ref,
                     dnw_ref, s0_ref, *rest, emit_o, unroll):
    if emit_o:
        o_ref, sout_ref = rest[0], rest[1]
        mq_ref, n_ref, oacc_ref = rest[2:]
    else:
        sout_ref = rest[0]
        mq_ref, n_ref = rest[1:]
    t = q_ref.shape[0]
    nblk = t // CHUNK
    h = pl.program_id(1)

    row_i = lax.broadcasted_iota(jnp.int32, (CHUNK, CHUNK), 0)
    col_i = lax.broadcasted_iota(jnp.int32, (CHUNK, CHUNK), 1)
    lane = col_i
    tri_lo = (row_i >= col_i).astype(BF16)
    a_neg = -jnp.exp(alog_ref[...])
    dtb = dtb_ref[...]
    xor = row_i ^ col_i
    lvl = jnp.zeros((CHUNK, CHUNK), jnp.int32)
    for l in range(1, LOG2_CHUNK):
        lvl = jnp.where(xor >= (1 << l), l, lvl)
    eye = (row_i == col_i).astype(F32)
    sub8 = lax.broadcasted_iota(jnp.int32, (HEADS, CHUNK), 0)

    def block_terms(blk):
        r0 = pl.multiple_of(blk * CHUNK, CHUNK)
        q = _l2norm(_conv_block(q_ref, wq_ref, blk, nblk)) * (HEAD_DIM ** -0.5)
        k = _l2norm(_conv_block(k_ref, wk_ref, blk, nblk))
        v = _conv_block(v_ref, wv_ref, blk, nblk)
        raw = ba_ref[pl.ds(r0, CHUNK), :]
        gate = jnp.where(lane < 16, _sigmoid(raw), a_neg * _softplus(raw + dtb))
        return dict(blk=blk, r0=r0, q=q, k=k, v=v, kt=k.T, kb=k.astype(BF16), gate=gate)

    def chain_setup(bt, d):
        gc, gct, kk = bt["gc"], bt["gct"], bt["kk"]
        beta = jnp.sum(jnp.where(lane == 8 * d + h, gc, 0.0), axis=1, keepdims=True)
        gcol = jnp.sum(jnp.where(lane == 16 + 8 * d + h, gc, 0.0), axis=1, keepdims=True)
        grow = jnp.sum(jnp.where(sub8 == h, gct[16 + 8 * d:24 + 8 * d, :], 0.0),
                       axis=0, keepdims=True)
        if d == 0:
            incl, strict = row_i >= col_i, row_i > col_i
            glast = gcol[CHUNK - 1:CHUNK, :]
        else:
            incl, strict = row_i <= col_i, row_i < col_i
            glast = gcol[0:1, :]
        decay = jnp.exp(jnp.where(incl, gcol - grow, -jnp.inf))
        lmat = jnp.where(strict, beta * kk * decay, 0.0)
        egc = jnp.exp(gcol)
        rhs = jnp.concatenate([bt["k"] * (beta * egc), bt["v"] * beta], axis=1).astype(BF16)
        ktail = (bt["kt"] * jnp.exp(glast - grow)).astype(BF16)
        return dict(bt=bt, d=d, decay=decay, lmat=lmat, egc=egc, rhs=rhs, ktail=ktail, glast=glast,
                    tinv=eye - jnp.where(lvl == 0, lmat, 0.0))

    def pass_b(i, states):
        new = []
        for d in range(N_DIR):
            blk = i if d == 0 else nblk - 1 - i
            r = _dot(mq_ref[d, blk], states[d].astype(BF16))
            if emit_o:
                r0 = pl.multiple_of(blk * CHUNK, CHUNK)
                oacc_ref[d, pl.ds(r0, CHUNK), :] += r[CHUNK:]
            new.append(r[:CHUNK] + n_ref[d, blk])
        return tuple(new)

    half = max(unroll // 2, 1)

    def pass_a(i, states, with_b):
        pending = [(i - 1) * half + u for u in range(half)] if with_b else []

        def recurrence_step(st):
            return pass_b(pending.pop(0), st) if pending else st

        ids = ([i * half + u for u in range(half)] + [nblk - 1 - (i * half + u) for u in range(half)]
               if unroll > 1 else [i])
        blocks = [block_terms(b) for b in ids]
        for bt in blocks:
            bt["g3"] = _split3(bt["gate"])
        for bt in blocks:
            bt["pre"] = sum(_dot(tri_lo, g) for g in bt["g3"])
            bt["suf"] = bt["pre"][CHUNK - 1:CHUNK, :] - bt["pre"] + bt["gate"]
        states = recurrence_step(states)
        for bt in blocks:
            bt["kk"] = _dot_nt(bt["kb"], bt["kb"])
            if emit_o:
                bt["qkt"] = _dot_nt(bt["q"].astype(BF16), bt["kb"])
        for bt in blocks:
            bt["gc"] = jnp.where(lane < 16, bt["gate"], jnp.where(lane < 24, bt["pre"], bt["suf"]))
            bt["gct"] = bt["gc"].T
        chains = [chain_setup(bt, d) for bt in blocks for d in range(N_DIR)]
        for l in range(1, LOG2_CHUNK):
            for c in chains:
                c["tb"] = c["tinv"].astype(BF16)
                c["x"] = _dot(c["tb"], jnp.where(lvl == l, c["lmat"], 0.0).astype(BF16)).astype(BF16)
            for c in chains:
                c["tinv"] = c["tinv"] - _dot(c["x"], c["tb"])
            states = recurrence_step(states)
        for c in chains:
            c["wu"] = _dot(c["tinv"].astype(BF16), c["rhs"]).astype(BF16)
        for c in chains:
            c["ku"] = _dot(c["ktail"], c["wu"])
            if emit_o:
                c["qu"] = _dot((c["bt"]["qkt"] * c["decay"]).astype(BF16), c["wu"])
        for c in chains:
            d, bt = c["d"], c["bt"]
            m = eye * jnp.exp(c["glast"]) - c["ku"][:, :HEAD_DIM]
            n_ref[d, bt["blk"]] = c["ku"][:, HEAD_DIM:]
            if emit_o:
                qe = bt["q"] * c["egc"] - c["qu"][:, :HEAD_DIM]
                mq_ref[d, bt["blk"]] = jnp.concatenate([m, qe], axis=0).astype(BF16)
                oacc_ref[d, pl.ds(bt["r0"], CHUNK), :] = c["qu"][:, HEAD_DIM:]
            else:
                mq_ref[d, bt["blk"]] = m.astype(BF16)
        while pending:
            states = recurrence_step(states)
        return states

    n_iter = nblk // unroll
    states = (s0_ref[0], s0_ref[1])
    if unroll > 1:
        states = pass_a(0, states, False)
        states = lax.fori_loop(1, n_iter, lambda i, st: pass_a(i, st, True), states)
        done = (n_iter - 1) * half
    else:
        states = lax.fori_loop(0, n_iter, lambda i, st: pass_a(i, st, False), states)
        done = 0
    states = lax.fori_loop(done, nblk, pass_b, states, unroll=2)
    sout_ref[0] = states[0]
    sout_ref[1] = states[1]

    if emit_o:
        def fin(r, carry):
            r0 = pl.multiple_of(r * CHUNK, CHUNK)
            o = oacc_ref[0, pl.ds(r0, CHUNK), :] + oacc_ref[1, pl.ds(r0, CHUNK), :]
            on = o * lax.rsqrt(jnp.mean(o * o, axis=-1, keepdims=True) + NORM_EPS) * dnw_ref[...]
            z = z_ref[pl.ds(r0, CHUNK), :].astype(F32)
            o_ref[pl.ds(r0, CHUNK), :] = (on * _silu(z)).astype(BF16)
            return carry

        lax.fori_loop(0, nblk, fin, 0, unroll=unroll)


def _deltanet(p, ba, convw, alog_row, dtb_row, dnw, s0, emit_o):
    b, t, _ = p.shape
    nblk = t // CHUNK
    unroll = 8 if nblk % 8 == 0 else (2 if nblk % 2 == 0 else 1)
    tok = lambda c0: pl.BlockSpec((None, t, 128), lambda i, j: (i, 0, c0 + j))
    cw = lambda c0: pl.BlockSpec((3, 128), lambda i, j: (0, c0 + j))
    par = pl.BlockSpec((1, 128), lambda i, j: (0, 0))
    st = pl.BlockSpec((N_DIR, None, None, HEAD_DIM, HEAD_DIM), lambda i, j: (0, i, j, 0, 0))
    out_specs = [st]
    out_shape = [jax.ShapeDtypeStruct((N_DIR, b, HEADS, HEAD_DIM, HEAD_DIM), F32)]
    scratch = [pltpu.VMEM((N_DIR, nblk, (2 if emit_o else 1) * CHUNK, HEAD_DIM), BF16),
               pltpu.VMEM((N_DIR, nblk, HEAD_DIM, HEAD_DIM), F32)]
    if emit_o:
        out_specs = [pl.BlockSpec((None, t, 128), lambda i, j: (i, 0, j))] + out_specs
        out_shape = [jax.ShapeDtypeStruct((b, t, DN_WIDTH), BF16)] + out_shape
        scratch.append(pltpu.VMEM((N_DIR, t, HEAD_DIM), F32))
    return pl.pallas_call(
        functools.partial(_deltanet_kernel, emit_o=emit_o, unroll=unroll),
        grid=(b, HEADS),
        in_specs=[tok(PCOL_Q), tok(PCOL_K), tok(PCOL_V), tok(PCOL_Z),
                  pl.BlockSpec((None, t, 128), lambda i, j: (i, 0, 0)),
                  cw(0), cw(HEADS), cw(2 * HEADS), par, par, par, st],
        out_specs=out_specs,
        out_shape=out_shape,
        scratch_shapes=scratch,
        compiler_params=_cparams("arbitrary", "arbitrary"),
        name="deltanet_x" if emit_o else "deltanet_ctx",
    )(p, p, p, p, ba, convw, convw, convw, alog_row, dtb_row, dnw, s0)


def _fourier_kernel(x_ref, wc_ref, twc_ref, tws_ref, cn_ref, sn_ref, o_ref, y_ref, a_ref, *, rows):
    n1 = FFT_N1
    t = x_ref.shape[0]
    n2 = t // n1
    fw = F_WIDTH
    wc = wc_ref[...].astype(BF16)
    for t1 in range(n1):
        y_ref[t1] = _dot(x_ref[t1 * n2:(t1 + 1) * n2, :], wc).astype(BF16)

    ang = [2.0 * math.pi * m / n1 for m in range(n1)]
    snap = lambda val: float(round(val)) if abs(val - round(val)) < 1e-9 else val
    cs = [(snap(math.cos(a)), snap(math.sin(a))) for a in ang]

    def axpy(acc, coef, val):
        if coef == 0.0:
            return acc
        if acc is None:
            return val if coef == 1.0 else (-val if coef == -1.0 else coef * val)
        if coef == 1.0:
            return acc + val
        if coef == -1.0:
            return acc - val
        return acc + coef * val

    hw = 256

    def stage1(rb, carry):
        r0 = pl.multiple_of(rb * rows, rows)
        for l0 in range(0, fw, hw):
            yr = [y_ref[t1, pl.ds(r0, rows), l0:l0 + hw].astype(F32) for t1 in range(n1)]
            yi = [y_ref[t1, pl.ds(r0, rows), fw + l0:fw + l0 + hw].astype(F32) for t1 in range(n1)]
            for k1 in range(n1):
                ar = None
                ai = None
                for t1 in range(n1):
                    c, s = cs[(t1 * k1) % n1]
                    ar = axpy(axpy(ar, c, yr[t1]), s, yi[t1])
                    ai = axpy(axpy(ai, c, yi[t1]), -s, yr[t1])
                twc = jnp.concatenate([twc_ref[k1, pl.ds(r0, rows), :]] * (hw // 128), axis=1)
                tws = jnp.concatenate([tws_ref[k1, pl.ds(r0, rows), :]] * (hw // 128), axis=1)
                a_ref[k1, 0, pl.ds(r0, rows), l0:l0 + hw] = (ar * twc + ai * tws).astype(BF16)
                a_ref[k1, 1, pl.ds(r0, rows), l0:l0 + hw] = (ai * twc - ar * tws).astype(BF16)
        return carry

    lax.fori_loop(0, n2 // rows, stage1, 0)

    cn = cn_ref[...].astype(BF16)
    sn = sn_ref[...].astype(BF16)
    for k1 in range(n1):
        zr = _dot(cn, a_ref[k1, 0]) + _dot(sn, a_ref[k1, 1])
        for g in range(N_GROUPS):
            o_ref[g, pl.ds(k1, n2, stride=n1), :] = zr[:, g * GROUP_DIM:(g + 1) * GROUP_DIM]


def _fourier_consts(t):
    n1 = FFT_N1
    n2 = t // n1
    scale = 1.0 / math.sqrt(t * GROUP_DIM)
    j = np.arange(GROUP_DIM)
    ang = 2.0 * np.pi * np.outer(j, j) / GROUP_DIM
    wc = np.zeros((F_WIDTH, 2 * F_WIDTH), np.float32)
    for g in range(N_GROUPS):
        sl = slice(g * GROUP_DIM, (g + 1) * GROUP_DIM)
        wc[sl, sl] = np.cos(ang) * scale
        wc[sl, F_WIDTH + g * GROUP_DIM:F_WIDTH + (g + 1) * GROUP_DIM] = -np.sin(ang) * scale
    t2 = np.arange(n2)
    k1 = np.arange(n1)
    tw = 2.0 * np.pi * np.outer(k1, t2) / t
    a2 = 2.0 * np.pi * np.outer(t2, t2) / n2
    return (wc, np.cos(tw).astype(np.float32), np.sin(tw).astype(np.float32),
            np.cos(a2).astype(np.float32), np.sin(a2).astype(np.float32))


def _fourier(p):
    b, t, _ = p.shape
    n1 = FFT_N1
    n2 = t // n1
    wc, twc, tws, cn, sn = _fourier_consts(t)
    wc, cn, sn = jnp.asarray(wc), jnp.asarray(cn), jnp.asarray(sn)
    twc = jnp.broadcast_to(jnp.asarray(twc)[:, :, None], (n1, n2, 128))
    tws = jnp.broadcast_to(jnp.asarray(tws)[:, :, None], (n1, n2, 128))
    return pl.pallas_call(
        functools.partial(_fourier_kernel, rows=16),
        grid=(b,),
        in_specs=[pl.BlockSpec((None, t, F_WIDTH), lambda i: (i, 0, PCOL_F // 4)),
                  _resident((F_WIDTH, 2 * F_WIDTH)),
                  _resident((n1, n2, 128)), _resident((n1, n2, 128)),
                  _resident((n2, n2)), _resident((n2, n2))],
        out_specs=pl.BlockSpec((None, N_GROUPS, t, GROUP_DIM), lambda i: (i, 0, 0, 0)),
        out_shape=jax.ShapeDtypeStruct((b, N_GROUPS, t, GROUP_DIM), F32),
        scratch_shapes=[pltpu.VMEM((n1, n2, 2 * F_WIDTH), BF16),
                        pltpu.VMEM((n1, 2, n2, F_WIDTH), BF16)],
        compiler_params=_cparams("arbitrary"),
        name="fourier",
    )(p, wc, twc, tws, cn, sn)


def _merge_kernel(x_ref, g0_ref, g1_ref, o_ref, f_ref, gt_ref, nw_ref, wf_ref, wd_ref, wo_ref, out_ref,
                  *, rows):
    tm = x_ref.shape[0]
    for r0 in range(0, tm, rows):
        sl = slice(r0, r0 + rows)
        fo = jnp.concatenate([f_ref[g, sl, :] for g in range(N_GROUPS)], axis=1).astype(BF16)
        yf = _dot(fo, wf_ref[...])
        yd = _dot(o_ref[sl, :], wd_ref[...])
        m = _sigmoid(g0_ref[sl, :].astype(F32)) * yf + _sigmoid(g1_ref[sl, :].astype(F32)) * yd
        y = _dot(m.astype(BF16), wo_ref[...])
        yn = y * lax.rsqrt(jnp.mean(y * y, axis=-1, keepdims=True) + NORM_EPS) * nw_ref[...]
        out_ref[sl, :] = x_ref[sl, :] + gt_ref[...] * yn


def _merge(x, p, o, four, mod3, nw, wf, wd, wo, tm):
    b, t, d = x.shape
    return pl.pallas_call(
        functools.partial(_merge_kernel, rows=256),
        grid=(b, t // tm),
        in_specs=[pl.BlockSpec((None, tm, d), lambda i, j: (i, j, 0)),
                  pl.BlockSpec((None, tm, d), lambda i, j: (i, j, PCOL_G0 // 8)),
                  pl.BlockSpec((None, tm, d), lambda i, j: (i, j, PCOL_G1 // 8)),
                  pl.BlockSpec((None, tm, DN_WIDTH), lambda i, j: (i, j, 0)),
                  pl.BlockSpec((None, N_GROUPS, tm, GROUP_DIM), lambda i, j: (i, 0, j, 0)),
                  pl.BlockSpec((None, 1, d), lambda i, j: (i, 0, 2)),
                  pl.BlockSpec((1, d), lambda i, j: (0, 0)),
                  _resident(wf.shape), _resident(wd.shape), _resident(wo.shape)],
        out_specs=pl.BlockSpec((None, tm, d), lambda i, j: (i, j, 0)),
        out_shape=jax.ShapeDtypeStruct((b, t, d), F32),
        compiler_params=_cparams("arbitrary", "arbitrary"),
        name="merge",
    )(x, p, p, o, four, mod3, nw, wf, wd, wo)


def _ffn_kernel(x_ref, xp_ref, xn_ref, sh_ref, sc_ref, gt_ref, npre_ref, npost_ref,
                wa_ref, wu_ref, cw_ref, wd_ref, out_ref, hb_ref, a_ref, a2_ref, acc_ref):
    tm = x_ref.shape[0]
    gw = GRID_W
    i = pl.program_id(1)
    last = pl.num_programs(1) - 1
    nf = wa_ref.shape[0]

    def nm(xt):
        return _norm_mod(xt, npre_ref[...], sh_ref[...], sc_ref[...]).astype(BF16)

    hb_ref[0:gw, :] = nm(xp_ref[...])
    for r0 in range(0, tm, 128):
        hb_ref[gw + r0:gw + r0 + 128, :] = nm(x_ref[r0:r0 + 128, :])
    hb_ref[gw + tm:, :] = nm(xn_ref[...])
    acc_ref[...] = jnp.zeros_like(acc_ref)
    top = jnp.where(i > 0, 1.0, 0.0)
    bot = jnp.where(i < last, 1.0, 0.0)
    colid = lax.broadcasted_iota(jnp.int32, (tm, 1), 0) % gw
    has_l = colid != 0
    has_r = colid != gw - 1

    def up_a(dst_ref, fj):
        a = _dot(hb_ref[...], wa_ref[fj])
        dst_ref[0:gw, :] = a[0:gw] * top
        dst_ref[gw:gw + tm, :] = a[gw:gw + tm]
        dst_ref[gw + tm:, :] = a[gw + tm:] * bot

    def consume(src_ref, fj):
        cw = cw_ref[fj]
        s0 = src_ref[0:tm, :]
        s1 = src_ref[gw:gw + tm, :]
        s2 = src_ref[2 * gw:2 * gw + tm, :]
        left = cw[0:1] * s0 + cw[3:4] * s1 + cw[6:7] * s2
        mid = cw[1:2] * s0 + cw[4:5] * s1 + cw[7:8] * s2
        right = cw[2:3] * s0 + cw[5:6] * s1 + cw[8:9] * s2
        conv = (mid + jnp.where(has_l, pltpu.roll(left, 1, 0), 0.0)
                + jnp.where(has_r, pltpu.roll(right, tm - 1, 0), 0.0))
        u = _dot(hb_ref[gw:gw + tm, :], wu_ref[fj])
        act = (_silu(conv) * u).astype(BF16)
        acc_ref[...] += _dot(act, wd_ref[fj])

    up_a(a_ref, 0)

    def body(i, carry):
        f0 = 2 * i
        up_a(a2_ref, f0 + 1)
        consume(a_ref, f0)
        up_a(a_ref, jnp.minimum(f0 + 2, nf - 1))
        consume(a2_ref, f0 + 1)
        return carry

    lax.fori_loop(0, nf // 2, body, 0)
    if nf % 2 == 1:
        consume(a_ref, nf - 1)
    for r0 in range(0, tm, 128):
        sl = slice(r0, r0 + 128)
        y = acc_ref[sl, :]
        yn = y * lax.rsqrt(jnp.mean(y * y, axis=-1, keepdims=True) + NORM_EPS) * npost_ref[...]
        out_ref[sl, :] = x_ref[sl, :] + gt_ref[...] * yn


def _ffn(x, mod3, npre, npost, wa, wu, cw, wd, tm):
    b, t, d = x.shape
    gw = GRID_W
    rpt = tm // gw
    nrow = t // gw
    nf, _, tf = wa.shape
    return pl.pallas_call(
        _ffn_kernel,
        grid=(b, t // tm),
        in_specs=[pl.BlockSpec((None, tm, d), lambda i, j: (i, j, 0)),
                  pl.BlockSpec((None, gw, d), lambda i, j: (i, jnp.maximum(j * rpt - 1, 0), 0)),
                  pl.BlockSpec((None, gw, d), lambda i, j: (i, jnp.minimum(j * rpt + rpt, nrow - 1), 0)),
                  pl.BlockSpec((None, 1, d), lambda i, j: (i, 0, 3)),
                  pl.BlockSpec((None, 1, d), lambda i, j: (i, 0, 4)),
                  pl.BlockSpec((None, 1, d), lambda i, j: (i, 0, 5)),
                  pl.BlockSpec((1, d), lambda i, j: (0, 0)),
                  pl.BlockSpec((1, d), lambda i, j: (0, 0)),
                  _resident(wa.shape), _resident(wu.shape), _resident(cw.shape), _resident(wd.shape)],
        out_specs=pl.BlockSpec((None, tm, d), lambda i, j: (i, j, 0)),
        out_shape=jax.ShapeDtypeStruct((b, t, d), F32),
        scratch_shapes=[pltpu.VMEM((tm + 2 * gw, d), BF16),
                        pltpu.VMEM((tm + 2 * gw, tf), F32),
                        pltpu.VMEM((tm + 2 * gw, tf), F32),
                        pltpu.VMEM((tm, d), F32)],
        compiler_params=_cparams("arbitrary", "arbitrary"),
        name="ffn",
    )(x, x, x, mod3, mod3, mod3, npre, npost, wa, wu, cw, wd)


def kernel(x, c, ctx, c_ctx, w_ada, b_ada, norm_pre_mix, norm_post_mix, norm_pre_ffn, norm_post_ffn,
           w_in, conv_qkv, a_log, dt_bias, dn_norm, w_fourier, w_dn, w_out, w_up, conv_ffn, w_down):
    assert w_ada.shape[0] == 1, "single-layer stack"
    bsz, seq, d = x.shape
    d_ff = w_down.shape[1]
    assert d == DN_WIDTH and seq % (FFT_N1 * 16) == 0 and seq % 512 == 0 and ctx.shape[1] % CHUNK == 0

    pad = (-(bsz + 1)) % 8
    c_rows = jnp.concatenate([c, c_ctx[None, :], jnp.zeros((pad, d), F32)], axis=0)
    mod = _ada(c_rows, w_ada[0], b_ada)
    mod3 = mod.reshape(mod.shape[0], 1, 6 * d)

    wi = w_in[0]
    o_q = F_WIDTH
    o_z = o_q + 3 * DN_WIDTH
    o_b = o_z + DN_WIDTH
    o_g = o_b + 2 * N_DIR * HEADS
    w_main = jnp.concatenate([wi[:, o_g:], wi[:, o_z:o_b], wi[:, o_q:o_z], wi[:, :o_q]], axis=1).astype(BF16)
    w_ba = jnp.pad(wi[:, o_b:o_g], ((0, 0), (0, 128 - 2 * N_DIR * HEADS))).astype(BF16)
    npm = norm_pre_mix
    p_c, ba_c = _inproj(ctx, mod3, lambda i: bsz, npm, w_main, w_ba, tm=ctx.shape[1])
    p_x, ba_x = _inproj(x, mod3, lambda i: i, npm, w_main, w_ba, tm=512)

    lanes = jnp.zeros((1, 128), F32)
    alog_row = lanes.at[0, 16:32].set(a_log[0].reshape(-1))
    dtb_row = lanes.at[0, 16:32].set(dt_bias[0].reshape(-1))
    zero = jnp.zeros((N_DIR, bsz, HEADS, HEAD_DIM, HEAD_DIM), F32)
    (s_ctx,) = _deltanet(p_c, ba_c, conv_qkv[0], alog_row, dtb_row, dn_norm, zero, emit_o=False)
    o_x, _ = _deltanet(p_x, ba_x, conv_qkv[0], alog_row, dtb_row, dn_norm, s_ctx, emit_o=True)

    four = _fourier(p_x)
    x1 = _merge(x, p_x, o_x, four, mod3, norm_post_mix, w_fourier[0].astype(BF16),
                w_dn[0].astype(BF16), w_out[0].astype(BF16), tm=512)

    tf = 256
    nf = d_ff // tf
    wup = w_up[0].astype(BF16)
    wa = wup[:, :d_ff].reshape(d, nf, tf).transpose(1, 0, 2)
    wu = wup[:, d_ff:].reshape(d, nf, tf).transpose(1, 0, 2)
    cw = conv_ffn[0].reshape(9, nf, tf).transpose(1, 0, 2)
    wd = w_down[0].astype(BF16).reshape(nf, tf, d)
    return _ffn(x1, mod3, norm_pre_ffn, norm_post_ffn, wa, wu, cw, wd, tm=1024)
```
